```python
import jax, jax.numpy as jnp
from jax import lax
import numpy as np

D_MODEL = 1024
BATCH = 16
SEQ = 2048
DEPTH = 1

HEAD_DIM = 64
GQA_HEADS = 8
GQA_KV_HEADS = 2
GQA_GROUP = GQA_HEADS // GQA_KV_HEADS
MLA_HEADS = 8
MLA_NOPE_DIM = 64
MLA_ROPE_DIM = 32
MLA_V_DIM = 64
MLA_Q_RANK = 256
MLA_KV_RANK = 256
MIX_WIDTH = GQA_HEADS * HEAD_DIM + MLA_HEADS * MLA_V_DIM
SZ_GQA_Q = GQA_HEADS * HEAD_DIM
SZ_GQA_K = GQA_KV_HEADS * HEAD_DIM
SZ_GQA_V = GQA_KV_HEADS * HEAD_DIM
SZ_MLA_CQ = MLA_Q_RANK
SZ_MLA_CKV = MLA_KV_RANK
SZ_MLA_KR = MLA_ROPE_DIM
IN_WIDTH = SZ_GQA_Q + SZ_GQA_K + SZ_GQA_V + SZ_MLA_CQ + SZ_MLA_CKV + SZ_MLA_KR
SPLIT_1 = SZ_GQA_Q
SPLIT_2 = SPLIT_1 + SZ_GQA_K
SPLIT_3 = SPLIT_2 + SZ_GQA_V
SPLIT_4 = SPLIT_3 + SZ_MLA_CQ
SPLIT_5 = SPLIT_4 + SZ_MLA_CKV
N_EXPERTS = 16
CAPACITY_FACTOR = 2
EXPERT_FF = 2816
PLE_DIM = 256
GRID_W = 64
Q_BLOCK = 128
ROPE_BASE = 10000.0
LN_EPS = 1e-5
RMS_EPS = 1e-6
DN_ALPHA = (2 * DEPTH) ** 0.25
DN_BETA = (8 * DEPTH) ** -0.25

kernel_name = 'hybrid_gqa_mla_ec_moe_encoder'


def layer_norm(x, g, b):
    xf = x.astype(jnp.float32)
    mu = jnp.mean(xf, axis=-1, keepdims=True)
    xc = xf - mu
    var = jnp.mean(xc * xc, axis=-1, keepdims=True)
    return (xc * lax.rsqrt(var + LN_EPS) * g.astype(jnp.float32) + b.astype(jnp.float32)).astype(x.dtype)


def rms_norm(x, g):
    xf = x.astype(jnp.float32)
    ms = jnp.mean(xf * xf, axis=-1, keepdims=True)
    return (xf * lax.rsqrt(ms + RMS_EPS) * g.astype(jnp.float32)).astype(x.dtype)


def rope_1d(x, pos):
    d = x.shape[-1]
    half = d // 2
    inv_freq = ROPE_BASE ** (-jnp.arange(half, dtype=jnp.float32) * 2.0 / d)
    ang = pos[:, None] * inv_freq[None, :]
    cos = jnp.cos(ang)[:, None, :]
    sin = jnp.sin(ang)[:, None, :]
    xf = x.astype(jnp.float32)
    x1, x2 = xf[..., :half], xf[..., half:]
    return jnp.concatenate([x1 * cos - x2 * sin, x2 * cos + x1 * sin], axis=-1).astype(x.dtype)


def axial_rope(x, row, col):
    h = x.shape[-1] // 2
    return jnp.concatenate([rope_1d(x[..., :h], row), rope_1d(x[..., h:], col)], axis=-1)


def block_attention(q, k, v, scale):
    B, S, KH, G, Dk = q.shape
    nb = S // Q_BLOCK
    qb = q.reshape(B, nb, Q_BLOCK, KH, G, Dk).transpose(1, 0, 2, 3, 4, 5)

    def one_block(qblk):
        s = jnp.einsum('bqhgd,bkhd->bhgqk', qblk, k, preferred_element_type=jnp.float32) * scale
        w = jax.nn.softmax(s, axis=-1).astype(v.dtype)
        return jnp.einsum('bhgqk,bkhd->bqhgd', w, v)

    o = lax.map(one_block, qb)
    return o.transpose(1, 0, 2, 3, 4, 5).reshape(B, S, KH, G, v.shape[-1])


def setup_inputs(seed: int = 0) -> dict:
    key = jax.random.key(seed)
    ks = jax.random.split(key, 24)
    f32 = jnp.float32

    def nrm(k, shape, scale):
        return jax.random.normal(k, shape, f32) * scale

    def gain(k, shape):
        return 1.0 + 0.02 * jax.random.normal(k, shape, f32)

    L = DEPTH
    return {
        'x': jax.random.normal(ks[0], (BATCH, SEQ, D_MODEL), f32),
        'p': jax.random.normal(ks[1], (DEPTH, BATCH, SEQ, PLE_DIM), f32),
        'w_in': nrm(ks[2], (L, D_MODEL, IN_WIDTH), D_MODEL ** -0.5),
        'q_norm': gain(ks[3], (L, HEAD_DIM)),
        'k_norm': gain(ks[4], (L, HEAD_DIM)),
        'cq_norm': gain(ks[5], (L, MLA_Q_RANK)),
        'ckv_norm': gain(ks[6], (L, MLA_KV_RANK)),
        'w_uq': nrm(ks[7], (L, MLA_Q_RANK, MLA_HEADS * (MLA_NOPE_DIM + MLA_ROPE_DIM)), MLA_Q_RANK ** -0.5),
        'w_ukv': nrm(ks[8], (L, MLA_KV_RANK, MLA_HEADS * (MLA_NOPE_DIM + MLA_V_DIM)), MLA_KV_RANK ** -0.5),
        'w_out': nrm(ks[9], (L, MIX_WIDTH, D_MODEL), DN_BETA * MIX_WIDTH ** -0.5),
        'ln_attn_g': gain(ks[10], (L, D_MODEL)),
        'ln_attn_b': nrm(ks[11], (L, D_MODEL), 0.02),
        'w_router': nrm(ks[12], (L, D_MODEL, N_EXPERTS), D_MODEL ** -0.5),
        'w_gate': nrm(ks[13], (L, N_EXPERTS, D_MODEL, EXPERT_FF), D_MODEL ** -0.5),
        'w_up': nrm(ks[14], (L, N_EXPERTS, D_MODEL, EXPERT_FF), D_MODEL ** -0.5),
        'w_down': nrm(ks[15], (L, N_EXPERTS, EXPERT_FF, D_MODEL), DN_BETA * EXPERT_FF ** -0.5),
        'ln_ffn_g': gain(ks[16], (L, D_MODEL)),
        'ln_ffn_b': nrm(ks[17], (L, D_MODEL), 0.02),
        'w_ple_proj': nrm(ks[18], (L, PLE_DIM, D_MODEL), DN_BETA * PLE_DIM ** -0.5),
        'w_ple_gate': nrm(ks[19], (L, D_MODEL, D_MODEL), D_MODEL ** -0.5),
        'ln_ple_g': gain(ks[20], (L, D_MODEL)),
        'ln_ple_b': nrm(ks[21], (L, D_MODEL), 0.02),
    }


def reference(x, p, w_in, q_norm, k_norm, cq_norm, ckv_norm, w_uq, w_ukv, w_out,
              ln_attn_g, ln_attn_b, w_router, w_gate, w_up, w_down, ln_ffn_g, ln_ffn_b,
              w_ple_proj, w_ple_gate, ln_ple_g, ln_ple_b):
    B, S, _ = x.shape
    rows = S // GRID_W
    row = jnp.repeat(jnp.arange(rows, dtype=jnp.float32), GRID_W)
    col = jnp.tile(jnp.arange(GRID_W, dtype=jnp.float32), rows)
    cap = CAPACITY_FACTOR * S // N_EXPERTS
    bidx = jnp.arange(B)[:, None, None]

    for i in range(DEPTH):
        proj = jnp.einsum('bsd,dn->bsn', x, w_in[i])
        q_g, k_g, v_g, c_q, c_kv, k_r = jnp.split(
            proj, [SPLIT_1, SPLIT_2, SPLIT_3, SPLIT_4, SPLIT_5], axis=-1)

        q = axial_rope(rms_norm(q_g.reshape(B, S, GQA_HEADS, HEAD_DIM), q_norm[i]), row, col)
        k = axial_rope(rms_norm(k_g.reshape(B, S, GQA_KV_HEADS, HEAD_DIM), k_norm[i]), row, col)
        v = v_g.reshape(B, S, GQA_KV_HEADS, HEAD_DIM)
        q = q.reshape(B, S, GQA_KV_HEADS, GQA_GROUP, HEAD_DIM)
        o_gqa = block_attention(q, k, v, HEAD_DIM ** -0.5).reshape(B, S, GQA_HEADS * HEAD_DIM)

        qm = jnp.einsum('bsr,rn->bsn', rms_norm(c_q, cq_norm[i]), w_uq[i])
        qm = qm.reshape(B, S, MLA_HEADS, MLA_NOPE_DIM + MLA_ROPE_DIM)
        q_pe = axial_rope(qm[..., MLA_NOPE_DIM:], row, col)
        q_m = jnp.concatenate([qm[..., :MLA_NOPE_DIM], q_pe], axis=-1)[:, :, :, None, :]
        kv = jnp.einsum('bsr,rn->bsn', rms_norm(c_kv, ckv_norm[i]), w_ukv[i])
        kv = kv.reshape(B, S, MLA_HEADS, MLA_NOPE_DIM + MLA_V_DIM)
        k_pe = axial_rope(k_r[:, :, None, :], row, col)
        k_m = jnp.concatenate(
            [kv[..., :MLA_NOPE_DIM], jnp.broadcast_to(k_pe, (B, S, MLA_HEADS, MLA_ROPE_DIM))], axis=-1)
        v_m = kv[..., MLA_NOPE_DIM:]
        o_mla = block_attention(q_m, k_m, v_m, (MLA_NOPE_DIM + MLA_ROPE_DIM) ** -0.5)
        o_mla = o_mla.reshape(B, S, MLA_HEADS * MLA_V_DIM)

        mix = jnp.einsum('bsm,md->bsd', jnp.concatenate([o_gqa, o_mla], axis=-1), w_out[i])
        x = layer_norm(DN_ALPHA * x + mix, ln_attn_g[i], ln_attn_b[i])

        logits = jnp.einsum('bsd,de->bse', x, w_router[i], preferred_element_type=jnp.float32)
        aff = jax.nn.softmax(logits, axis=-1).transpose(0, 2, 1)
        g_val, t_idx = lax.top_k(aff, cap)
        xg = x[bidx, t_idx]
        h = jax.nn.silu(jnp.einsum('becd,edf->becf', xg, w_gate[i])) * \
            jnp.einsum('becd,edf->becf', xg, w_up[i])
        y = jnp.einsum('becf,efd->becd', h, w_down[i]) * g_val[..., None].astype(x.dtype)
        moe = jnp.zeros_like(x).at[bidx, t_idx].add(y)
        x = layer_norm(DN_ALPHA * x + moe, ln_ffn_g[i], ln_ffn_b[i])

        e = jnp.einsum('bsp,pd->bsd', p[i], w_ple_proj[i])
        gate = jax.nn.sigmoid(jnp.einsum('bsd,de->bse', x, w_ple_gate[i]))
        x = layer_norm(DN_ALPHA * x + gate * e, ln_ple_g[i], ln_ple_b[i])

    return x
```

```python
import functools

import numpy as np
import jax
import jax.numpy as jnp
from jax import lax
from jax.experimental import pallas as pl
from jax.experimental.pallas import tpu as pltpu

F32 = jnp.float32
BF16 = jnp.bfloat16

D_MODEL = 1024
HEAD_DIM = 64
GQA_HEADS = 8
GQA_KV_HEADS = 2
MLA_HEADS = 8
MLA_NOPE_DIM = 64
MLA_ROPE_DIM = 32
MLA_V_DIM = 64
MLA_Q_RANK = 256
MLA_KV_RANK = 256
N_EXPERTS = 16
CAPACITY_FACTOR = 2
EXPERT_FF = 2816
PLE_DIM = 256
GRID_W = 64
ROPE_BASE = 10000.0
LN_EPS = 1e-5
RMS_EPS = 1e-6
DEPTH = 1
DN_ALPHA = (2 * DEPTH) ** 0.25

LANES = 128
SZ_Q = GQA_HEADS * HEAD_DIM
SZ_KV = GQA_KV_HEADS * HEAD_DIM
IN_COLS = SZ_Q + 2 * SZ_KV + MLA_Q_RANK + MLA_KV_RANK + LANES
OFF_K = SZ_Q
OFF_V = OFF_K + SZ_KV
OFF_CQ = OFF_V + SZ_KV
OFF_CKV = OFF_CQ + MLA_Q_RANK
OFF_KR = OFF_CKV + MLA_KV_RANK
N_CHUNKS = 12
N_PAIRS = 8
QKV_COLS = N_CHUNKS * LANES
MIX_WIDTH = N_PAIRS * LANES

TM_IN = 512
TQ = 256
TM_OUT = 512
FF_TILE = 256
FFN_ROWS = 2048
FFN_RB = 256
TS_FIN = 512

VMEM_LIMIT = 56 * 1024 * 1024


def _cparams(n_axes, vmem=VMEM_LIMIT):
    return pltpu.CompilerParams(dimension_semantics=("arbitrary",) * n_axes, vmem_limit_bytes=vmem)


def _layer_norm(y, g, b):
    mu = jnp.mean(y, axis=-1, keepdims=True)
    yc = y - mu
    var = jnp.mean(yc * yc, axis=-1, keepdims=True)
    return yc * lax.rsqrt(var + LN_EPS) * g + b


def _split_bf16(v):
    hi = v.astype(BF16)
    lo = (v - hi.astype(F32)).astype(BF16)
    return hi, lo


def _rope(v, cos, sin, dist):
    lane = lax.broadcasted_iota(jnp.int32, v.shape, 1)
    first = (lane % (2 * dist)) < dist
    rot = jnp.where(first, pltpu.roll(v, LANES - dist, 1), pltpu.roll(v, dist, 1))
    return v * cos + rot * sin


def _head_sumsq(v, g_ref):
    hi, lo = _split_bf16(v * v)
    g = g_ref[...]
    return jnp.dot(hi, g, preferred_element_type=F32) + jnp.dot(lo, g, preferred_element_type=F32)


def _inproj_kernel(x_ref, win_ref, wuq_ref, wukv_ref, g512_ref, g128_ref, qg_ref, kg_ref, cqg_ref, ckvg_ref,
                   cos64_ref, sin64_ref, cosm_ref, sinm_ref, q_out, k_out, v_out):
    proj = jnp.dot(x_ref[...].astype(BF16), win_ref[...], preferred_element_type=F32)
    cos64, sin64 = cos64_ref[...], sin64_ref[...]
    cosm, sinm = cosm_ref[...], sinm_ref[...]
    lane = lax.broadcasted_iota(jnp.int32, (proj.shape[0], LANES), 1)
    low = lane < HEAD_DIM

    qg = proj[:, :SZ_Q]
    qn = qg * lax.rsqrt(_head_sumsq(qg, g512_ref) * (1.0 / HEAD_DIM) + RMS_EPS) * qg_ref[...]
    for c in range(SZ_Q // LANES):
        sl = slice(c * LANES, (c + 1) * LANES)
        q_out[:, sl] = _rope(qn[:, sl], cos64, sin64, HEAD_DIM // 4).astype(BF16)

    kg = proj[:, OFF_K:OFF_K + SZ_KV]
    kn = kg * lax.rsqrt(_head_sumsq(kg, g128_ref) * (1.0 / HEAD_DIM) + RMS_EPS) * kg_ref[...]
    kr = _rope(kn, cos64, sin64, HEAD_DIM // 4)
    vg = proj[:, OFF_V:OFF_V + SZ_KV]
    for src, dst in ((kr, k_out), (vg, v_out)):
        lo = jnp.where(low, src, 0.0)
        hi = jnp.where(low, 0.0, src)
        dst[:, 0 * LANES:1 * LANES] = lo.astype(BF16)
        dst[:, 1 * LANES:2 * LANES] = pltpu.roll(lo, HEAD_DIM, 1).astype(BF16)
        dst[:, 2 * LANES:3 * LANES] = pltpu.roll(hi, HEAD_DIM, 1).astype(BF16)
        dst[:, 3 * LANES:4 * LANES] = hi.astype(BF16)

    cq = proj[:, OFF_CQ:OFF_CQ + MLA_Q_RANK]
    cqn = cq * lax.rsqrt(jnp.sum(cq * cq, axis=-1, keepdims=True) * (1.0 / MLA_Q_RANK) + RMS_EPS) * cqg_ref[...]
    qm = jnp.dot(cqn.astype(BF16), wuq_ref[...], preferred_element_type=F32)
    mla_scale = (MLA_NOPE_DIM + MLA_ROPE_DIM) ** -0.5
    for h in range(MLA_HEADS):
        sl = slice(h * LANES, (h + 1) * LANES)
        dsl = slice(SZ_Q + h * LANES, SZ_Q + (h + 1) * LANES)
        q_out[:, dsl] = (_rope(qm[:, sl], cosm, sinm, MLA_ROPE_DIM // 4) * mla_scale).astype(BF16)

    ckv = proj[:, OFF_CKV:OFF_CKV + MLA_KV_RANK]
    ckvn = ckv * lax.rsqrt(jnp.sum(ckv * ckv, axis=-1, keepdims=True) * (1.0 / MLA_KV_RANK) + RMS_EPS) * ckvg_ref[...]
    kv = jnp.dot(ckvn.astype(BF16), wukv_ref[...], preferred_element_type=F32)
    kpe = _rope(proj[:, OFF_KR:OFF_KR + LANES], cosm, sinm, MLA_ROPE_DIM // 4)
    for h in range(MLA_HEADS):
        sl = slice(h * LANES, (h + 1) * LANES)
        dsl = slice(SZ_Q + h * LANES, SZ_Q + (h + 1) * LANES)
        k_out[:, dsl] = (kv[:, sl] + kpe).astype(BF16)
    voff = MLA_HEADS * LANES
    for j in range(MLA_HEADS // 2):
        vp = kv[:, voff + j * LANES: voff + (j + 1) * LANES]
        base = SZ_Q + 2 * j * LANES
        v_out[:, base:base + LANES] = jnp.where(low, vp, 0.0).astype(BF16)
        v_out[:, base + LANES:base + 2 * LANES] = jnp.where(low, 0.0, vp).astype(BF16)


def _inproj_call(x2d, win, wuq, wukv, g512, g128, qg, kg, cqg, ckvg, cos64, sin64, cosm, sinm, seq):
    T = x2d.shape[0]
    tm = TM_IN
    tiles_per_seq = seq // tm
    full = lambda shape: pl.BlockSpec(shape, lambda i: (0,) * len(shape))
    tab = pl.BlockSpec((tm, LANES), lambda i: (i % tiles_per_seq, 0))
    out_spec = pl.BlockSpec((tm, QKV_COLS), lambda i: (i, 0))
    out_shape = jax.ShapeDtypeStruct((T, QKV_COLS), BF16)
    return pl.pallas_call(
        _inproj_kernel,
        grid=(T // tm,),
        in_specs=[pl.BlockSpec((tm, D_MODEL), lambda i: (i, 0)),
                  full(win.shape), full(wuq.shape), full(wukv.shape), full(g512.shape), full(g128.shape),
                  full(qg.shape), full(kg.shape), full(cqg.shape), full(ckvg.shape),
                  tab, tab, tab, tab],
        out_specs=[out_spec, out_spec, out_spec],
        out_shape=[out_shape, out_shape, out_shape],
        compiler_params=_cparams(1),
        name="inproj",
    )(x2d, win, wuq, wukv, g512, g128, qg, kg, cqg, ckvg, cos64, sin64, cosm, sinm)


def _attn_kernel(qe_ref, qo_ref, ke_ref, ko_ref, ve_ref, vo_ref, o_ref):
    seq = o_ref.shape[0]

    def head(q, k_ref, v_ref):
        s = lax.dot_general(q, k_ref[...], (((1,), (1,)), ((), ())), preferred_element_type=F32)
        m = jnp.max(s, axis=-1, keepdims=True)
        p = jnp.exp(s - m)
        l = jnp.sum(p, axis=-1, keepdims=True)
        o = jnp.dot(p.astype(BF16), v_ref[...], preferred_element_type=F32)
        return o / l

    def body(i, carry):
        r0 = pl.multiple_of(i * TQ, TQ)
        o = head(qe_ref[pl.ds(r0, TQ), :], ke_ref, ve_ref) + head(qo_ref[pl.ds(r0, TQ), :], ko_ref, vo_ref)
        o_ref[pl.ds(r0, TQ), :] = o.astype(BF16)
        return carry

    lax.fori_loop(0, seq // TQ, body, 0)


def _q_chunk(pr, odd):
    return jnp.where(pr < 4, pr, 4 + 2 * (pr - 4) + odd)


def _kv_chunk(pr, odd):
    return jnp.where(pr < 4, 2 * (pr // 2) + odd, 4 + 2 * (pr - 4) + odd)


def _attn_call(q_all, k_all, v_all, batch, seq):
    T = q_all.shape[0]
    blk = (seq, LANES)
    qspec = lambda odd: pl.BlockSpec(blk, lambda b, pr: (b, _q_chunk(pr, odd)))
    kvspec = lambda odd: pl.BlockSpec(blk, lambda b, pr: (b, _kv_chunk(pr, odd)))
    return pl.pallas_call(
        _attn_kernel,
        grid=(batch, N_PAIRS),
        in_specs=[qspec(0), qspec(1), kvspec(0), kvspec(1), kvspec(0), kvspec(1)],
        out_specs=pl.BlockSpec(blk, lambda b, pr: (b, pr)),
        out_shape=jax.ShapeDtypeStruct((T, MIX_WIDTH), BF16),
        compiler_params=_cparams(2),
        name="attention",
    )(q_all, q_all, k_all, k_all, v_all, v_all)


def _outproj_kernel(o_ref, x_ref, wout_ref, g_ref, b_ref, wrh_ref, wrl_ref, x1_ref, x1b_ref, aff_ref):
    mix = jnp.dot(o_ref[...], wout_ref[...], preferred_element_type=F32)
    x1 = _layer_norm(DN_ALPHA * x_ref[...] + mix, g_ref[...], b_ref[...])
    x1_ref[...] = x1
    hi, lo = _split_bf16(x1)
    x1b_ref[...] = hi
    nt = (((1,), (1,)), ((), ()))
    wrh, wrl = wrh_ref[...], wrl_ref[...]
    logits = (lax.dot_general(wrh, hi, nt, preferred_element_type=F32)
              + lax.dot_general(wrh, lo, nt, preferred_element_type=F32)
              + lax.dot_general(wrl, hi, nt, preferred_element_type=F32))
    e = jnp.exp(logits - jnp.max(logits, axis=0, keepdims=True))
    aff_ref[...] = e / jnp.sum(e, axis=0, keepdims=True)


def _outproj_call(o_mix, x2d, wout, g, b, wrh, wrl):
    T = x2d.shape[0]
    tm = TM_OUT
    full = lambda shape: pl.BlockSpec(shape, lambda i: (0,) * len(shape))
    row = lambda w: pl.BlockSpec((tm, w), lambda i: (i, 0))
    return pl.pallas_call(
        _outproj_kernel,
        grid=(T // tm,),
        in_specs=[row(MIX_WIDTH), row(D_MODEL), full(wout.shape), full(g.shape), full(b.shape),
                  full(wrh.shape), full(wrl.shape)],
        out_specs=[row(D_MODEL), row(D_MODEL), pl.BlockSpec((N_EXPERTS, tm), lambda i: (0, i))],
        out_shape=[jax.ShapeDtypeStruct((T, D_MODEL), F32), jax.ShapeDtypeStruct((T, D_MODEL), BF16),
                   jax.ShapeDtypeStruct((N_EXPERTS, T), F32)],
        compiler_params=_cparams(1),
        name="outproj_router",
    )(o_mix, x2d, wout, g, b, wrh, wrl)


def _excl_cumsum(mask, upper):
    n = mask.shape[1]
    ones = jnp.where(mask, 1.0, 0.0)
    run = jnp.zeros((mask.shape[0], 1), F32)
    outs = []
    for j in range(n // LANES):
        blk = ones[:, j * LANES:(j + 1) * LANES]
        outs.append(jnp.dot(blk.astype(BF16), upper, preferred_element_type=F32) + run)
        run = run + jnp.sum(blk, axis=1, keepdims=True)
    return jnp.concatenate(outs, axis=1)


def _topk_kernel(cap, aff_ref, pos_ref, post_ref):
    a = aff_ref[...]
    bits = lax.bitcast_convert_type(a, jnp.int32)
    n_e, seq = a.shape

    def step(k, t):
        cand = t | jnp.left_shift(jnp.int32(1), 30 - k)
        cnt = jnp.sum(jnp.where(bits >= cand, 1.0, 0.0), axis=1, keepdims=True)
        return jnp.where(cnt >= cap, cand, t)

    t = lax.fori_loop(0, 31, step, jnp.zeros((n_e, 1), jnp.int32))
    gt = bits > t
    eq = bits == t
    need = cap - jnp.sum(jnp.where(gt, 1.0, 0.0), axis=1, keepdims=True)
    ri = lax.broadcasted_iota(jnp.int32, (LANES, LANES), 0)
    ci = lax.broadcasted_iota(jnp.int32, (LANES, LANES), 1)
    upper = jnp.where(ri < ci, 1.0, 0.0).astype(BF16)
    sel = gt | (eq & (_excl_cumsum(eq, upper) < need))
    pos = jnp.where(sel, _excl_cumsum(sel, upper), -1.0)
    pos_ref[0] = pos
    padded = jnp.concatenate([pos, jnp.full((LANES - n_e, seq), -1.0, F32)], axis=0)
    post_ref[0] = padded.T


def _topk_call(aff_t, batch, seq, cap):
    return pl.pallas_call(
        functools.partial(_topk_kernel, cap),
        grid=(batch,),
        in_specs=[pl.BlockSpec((N_EXPERTS, seq), lambda b: (0, b))],
        out_specs=[pl.BlockSpec((1, N_EXPERTS, seq), lambda b: (b, 0, 0)),
                   pl.BlockSpec((1, seq, LANES), lambda b: (b, 0, 0))],
        out_shape=[jax.ShapeDtypeStruct((batch, N_EXPERTS, seq), F32),
                   jax.ShapeDtypeStruct((batch, seq, LANES), F32)],
        compiler_params=_cparams(1),
        name="expert_choice",
    )(aff_t)


def _gather_kernel(cap, pos_ref, aff_ref, x_ref, xg_ref, gs_ref):
    posr = pos_ref[0, 0]
    seq = posr.shape[1]
    slot = lax.broadcasted_iota(jnp.int32, (cap, seq), 0).astype(F32)
    onehot = posr == slot
    xg = jnp.dot(jnp.where(onehot, 1.0, 0.0).astype(BF16), x_ref[...], preferred_element_type=F32)
    xg_ref[0] = xg.astype(BF16)
    g = jnp.sum(jnp.where(onehot, aff_ref[0, 0], 0.0), axis=1, keepdims=True)
    gs_ref[0] = jnp.broadcast_to(g, (cap, LANES))


def _gather_call(pos4, aff4, x1b, batch, seq, cap):
    return pl.pallas_call(
        functools.partial(_gather_kernel, cap),
        grid=(batch, N_EXPERTS),
        in_specs=[pl.BlockSpec((1, 1, 1, seq), lambda b, e: (b, e, 0, 0)),
                  pl.BlockSpec((1, 1, 1, seq), lambda b, e: (e, b, 0, 0)),
                  pl.BlockSpec((seq, D_MODEL), lambda b, e: (b, 0))],
        out_specs=[pl.BlockSpec((1, cap, D_MODEL), lambda b, e: (e, b, 0)),
                   pl.BlockSpec((1, cap, LANES), lambda b, e: (e, b, 0))],
        out_shape=[jax.ShapeDtypeStruct((N_EXPERTS, batch * cap, D_MODEL), BF16),
                   jax.ShapeDtypeStruct((N_EXPERTS, batch * cap, LANES), F32)],
        compiler_params=_cparams(2),
        name="moe_gather",
    )(pos4, aff4, x1b)


def _ffn_kernel(xg_ref, gs_ref, wg_ref, wu_ref, wd_ref, y_ref, acc_ref, wgb_ref, wub_ref, wdb_ref):
    f = pl.program_id(2)
    wgb_ref[...] = wg_ref[0].astype(BF16)
    wub_ref[...] = wu_ref[0].astype(BF16)
    wdb_ref[...] = wd_ref[0].astype(BF16)

    @pl.when(f == 0)
    def _():
        acc_ref[...] = jnp.zeros_like(acc_ref)

    def body(i, carry):
        r0 = pl.multiple_of(i * FFN_RB, FFN_RB)
        xb = xg_ref[0, pl.ds(r0, FFN_RB), :]
        g = jnp.dot(xb, wgb_ref[...], preferred_element_type=F32)
        u = jnp.dot(xb, wub_ref[...], preferred_element_type=F32)
        h = (g * jax.nn.sigmoid(g) * u).astype(BF16)
        acc_ref[pl.ds(r0, FFN_RB), :] += jnp.dot(h, wdb_ref[...], preferred_element_type=F32)
        return carry

    lax.fori_loop(0, FFN_ROWS // FFN_RB, body, 0)

    @pl.when(f == pl.num_programs(2) - 1)
    def _():
        y_ref[0] = (acc_ref[...] * gs_ref[0][:, :1]).astype(BF16)


def _ffn_call(xg, gs, w_gate, w_up, w_down):
    n_e, rows, _ = xg.shape
    return pl.pallas_call(
        _ffn_kernel,
        grid=(n_e, rows // FFN_ROWS, EXPERT_FF // FF_TILE),
        in_specs=[pl.BlockSpec((1, FFN_ROWS, D_MODEL), lambda e, m, f: (e, m, 0)),
                  pl.BlockSpec((1, FFN_ROWS, LANES), lambda e, m, f: (e, m, 0)),
                  pl.BlockSpec((1, D_MODEL, FF_TILE), lambda e, m, f: (e, 0, f)),
                  pl.BlockSpec((1, D_MODEL, FF_TILE), lambda e, m, f: (e, 0, f)),
                  pl.BlockSpec((1, FF_TILE, D_MODEL), lambda e, m, f: (e, f, 0))],
        out_specs=pl.BlockSpec((1, FFN_ROWS, D_MODEL), lambda e, m, f: (e, m, 0)),
        out_shape=jax.ShapeDtypeStruct((n_e, rows, D_MODEL), BF16),
        scratch_shapes=[pltpu.VMEM((FFN_ROWS, D_MODEL), F32),
                        pltpu.VMEM((D_MODEL, FF_TILE), BF16),
                        pltpu.VMEM((D_MODEL, FF_TILE), BF16),
                        pltpu.VMEM((FF_TILE, D_MODEL), BF16)],
        compiler_params=_cparams(3),
        name="moe_ffn",
    )(xg, gs, w_gate, w_up, w_down)


def _final_kernel(cap, post_ref, yg_ref, x1_ref, p_ref, wpp_ref, wpg_ref, g2_ref, b2_ref, g3_ref, b3_ref, out_ref):
    post = post_ref[0]
    ts = post.shape[0]
    slot = lax.broadcasted_iota(jnp.int32, (ts, cap), 1).astype(F32)
    moe = jnp.zeros((ts, D_MODEL), F32)
    for e in range(N_EXPERTS):
        onehot = jnp.where(post[:, e:e + 1] == slot, 1.0, 0.0).astype(BF16)
        moe = moe + jnp.dot(onehot, yg_ref[e], preferred_element_type=F32)
    x2 = _layer_norm(DN_ALPHA * x1_ref[...] + moe, g2_ref[...], b2_ref[...])
    emb = jnp.dot(p_ref[...].astype(BF16), wpp_ref[...], preferred_element_type=F32)
    gate = jax.nn.sigmoid(jnp.dot(x2.astype(BF16), wpg_ref[...], preferred_element_type=F32))
    out_ref[...] = _layer_norm(DN_ALPHA * x2 + gate * emb, g3_ref[...], b3_ref[...])


def _final_call(post, yg, x1, p2d, wpp, wpg, g2, b2, g3, b3, batch, seq, cap):
    T = x1.shape[0]
    ts = TS_FIN
    nt = seq // ts
    full = lambda shape: pl.BlockSpec(shape, lambda b, i: (0,) * len(shape))
    row = lambda w: pl.BlockSpec((ts, w), lambda b, i: (b * nt + i, 0))
    return pl.pallas_call(
        functools.partial(_final_kernel, cap),
        grid=(batch, nt),
        in_specs=[pl.BlockSpec((1, ts, LANES), lambda b, i: (b, i, 0)),
                  pl.BlockSpec((N_EXPERTS, cap, D_MODEL), lambda b, i: (0, b, 0)),
                  row(D_MODEL), row(PLE_DIM), full(wpp.shape), full(wpg.shape),
                  full(g2.shape), full(b2.shape), full(g3.shape), full(b3.shape)],
        out_specs=row(D_MODEL),
        out_shape=jax.ShapeDtypeStruct((T, D_MODEL), F32),
        compiler_params=_cparams(2),
        name="combine_ple",
    )(post, yg, x1, p2d, wpp, wpg, g2, b2, g3, b3)


def _rope_tables(seq):
    rows = seq // GRID_W
    row = jnp.repeat(jnp.arange(rows, dtype=F32), GRID_W)
    col = jnp.tile(jnp.arange(GRID_W, dtype=F32), rows)

    def table(head_dim, lane_of):
        sub = head_dim // 2
        half = sub // 2
        inv_freq = ROPE_BASE ** (-jnp.arange(half, dtype=F32) * 2.0 / sub)
        ang_row = row[:, None] * inv_freq[None, :]
        ang_col = col[:, None] * inv_freq[None, :]
        ang = jnp.concatenate([ang_row, ang_col], axis=1)
        idx = np.where(lane_of < 0, 0, (lane_of // sub) * half + (lane_of % sub) % half)
        first = (lane_of % sub) < half
        active = lane_of >= 0
        a = ang[:, idx]
        cos = jnp.where(active[None, :], jnp.cos(a), 1.0)
        sin = jnp.where(active[None, :], jnp.where(first[None, :], -jnp.sin(a), jnp.sin(a)), 0.0)
        return cos, sin

    lanes = np.arange(LANES)
    cos64, sin64 = table(HEAD_DIM, lanes % HEAD_DIM)
    mla_lane = np.where((lanes >= MLA_NOPE_DIM) & (lanes < MLA_NOPE_DIM + MLA_ROPE_DIM), lanes - MLA_NOPE_DIM, -1)
    cosm, sinm = table(MLA_ROPE_DIM, mla_lane)
    return cos64, sin64, cosm, sinm


def _block_ones(n, blk):
    i = np.arange(n) // blk
    return jnp.asarray((i[:, None] == i[None, :]).astype(np.float32), dtype=BF16)


def kernel(x, p, w_in, q_norm, k_norm, cq_norm, ckv_norm, w_uq, w_ukv, w_out, ln_attn_g, ln_attn_b, w_router,
           w_gate, w_up, w_down, ln_ffn_g, ln_ffn_b, w_ple_proj, w_ple_gate, ln_ple_g, ln_ple_b):
    assert w_in.shape[0] == DEPTH
    B, S, D = x.shape
    T = B * S
    cap = CAPACITY_FACTOR * S // N_EXPERTS
    x2d = x.reshape(T, D)

    wi = w_in[0]
    zc = lambda n: jnp.zeros((D, n), F32)
    win = jnp.concatenate([wi[:, :OFF_KR], zc(MLA_NOPE_DIM), wi[:, OFF_KR:], zc(LANES - MLA_NOPE_DIM - MLA_ROPE_DIM)],
                          axis=1).astype(BF16)
    qk = MLA_NOPE_DIM + MLA_ROPE_DIM
    wuq = jnp.pad(w_uq[0].reshape(MLA_Q_RANK, MLA_HEADS, qk), ((0, 0), (0, 0), (0, LANES - qk)))
    wuq = wuq.reshape(MLA_Q_RANK, MLA_HEADS * LANES).astype(BF16)
    wkv = w_ukv[0].reshape(MLA_KV_RANK, MLA_HEADS, MLA_NOPE_DIM + MLA_V_DIM)
    wk = jnp.pad(wkv[:, :, :MLA_NOPE_DIM], ((0, 0), (0, 0), (0, LANES - MLA_NOPE_DIM)))
    wukv = jnp.concatenate([wk.reshape(MLA_KV_RANK, MLA_HEADS * LANES),
                            wkv[:, :, MLA_NOPE_DIM:].reshape(MLA_KV_RANK, MLA_HEADS * MLA_V_DIM)], axis=1).astype(BF16)
    qg = (jnp.tile(q_norm[0], GQA_HEADS) * (HEAD_DIM ** -0.5)).reshape(1, SZ_Q)
    kg = jnp.tile(k_norm[0], GQA_KV_HEADS).reshape(1, SZ_KV)
    cqg = cq_norm[0].reshape(1, MLA_Q_RANK)
    ckvg = ckv_norm[0].reshape(1, MLA_KV_RANK)
    cos64, sin64, cosm, sinm = _rope_tables(S)

    q_all, k_all, v_all = _inproj_call(x2d, win, wuq, wukv, _block_ones(SZ_Q, HEAD_DIM), _block_ones(SZ_KV, HEAD_DIM),
                                       qg, kg, cqg, ckvg, cos64, sin64, cosm, sinm, S)
    o_mix = _attn_call(q_all, k_all, v_all, B, S)

    wr_t = w_router[0].T
    wrh = wr_t.astype(BF16)
    wrl = (wr_t - wrh.astype(F32)).astype(BF16)
    x1, x1b, aff_t = _outproj_call(o_mix, x2d, w_out[0].astype(BF16), ln_attn_g[0].reshape(1, D),
                                   ln_attn_b[0].reshape(1, D), wrh, wrl)

    pos, post = _topk_call(aff_t, B, S, cap)
    xg, gs = _gather_call(pos.reshape(B, N_EXPERTS, 1, S), aff_t.reshape(N_EXPERTS, B, 1, S), x1b, B, S, cap)
    yg = _ffn_call(xg, gs, w_gate[0], w_up[0], w_down[0])
    out = _final_call(post, yg, x1, p[0].reshape(T, PLE_DIM), w_ple_proj[0].astype(BF16), w_ple_gate[0].astype(BF16),
                      ln_ffn_g[0].reshape(1, D), ln_ffn_b[0].reshape(1, D), ln_ple_g[0].reshape(1, D),
                      ln_ple_b[0].reshape(1, D), B, S, cap)
    return out.reshape(B, S, D)
```

```python
import functools

import numpy as np
import jax
import jax.numpy as jnp
from jax import lax
from jax.experimental import pallas as pl
from jax.experimental.pallas import tpu as pltpu

F32 = jnp.float32
BF16 = jnp.bfloat16

D_MODEL = 1024
HEAD_DIM = 64
GQA_HEADS = 8
GQA_KV_HEADS = 2
MLA_HEADS = 8
MLA_NOPE_DIM = 64
MLA_ROPE_DIM = 32
MLA_V_DIM = 64
MLA_Q_RANK = 256
MLA_KV_RANK = 256
N_EXPERTS = 16
CAPACITY_FACTOR = 2
EXPERT_FF = 2816
PLE_DIM = 256
GRID_W = 64
ROPE_BASE = 10000.0
LN_EPS = 1e-5
RMS_EPS = 1e-6
DEPTH = 1
DN_ALPHA = (2 * DEPTH) ** 0.25

LANES = 128
SZ_Q = GQA_HEADS * HEAD_DIM
SZ_KV = GQA_KV_HEADS * HEAD_DIM
IN_COLS = SZ_Q + 2 * SZ_KV + MLA_Q_RANK + MLA_KV_RANK + LANES
OFF_K = SZ_Q
OFF_V = OFF_K + SZ_KV
OFF_CQ = OFF_V + SZ_KV
OFF_CKV = OFF_CQ + MLA_Q_RANK
OFF_KR = OFF_CKV + MLA_KV_RANK
N_CHUNKS = 12
N_PAIRS = 8
QKV_COLS = N_CHUNKS * LANES
MIX_WIDTH = N_PAIRS * LANES

TM_IN = 512
TQ = 256
TM_OUT = 512
FF_TILE = 256
FFN_ROWS = 2048
FFN_RB = 256
TS_FIN = 512

VMEM_LIMIT = 56 * 1024 * 1024


def _cparams(n_axes, vmem=VMEM_LIMIT):
    return pltpu.CompilerParams(dimension_semantics=("arbitrary",) * n_axes, vmem_limit_bytes=vmem)


def _layer_norm(y, g, b):
    mu = jnp.mean(y, axis=-1, keepdims=True)
    yc = y - mu
    var = jnp.mean(yc * yc, axis=-1, keepdims=True)
    return yc * lax.rsqrt(var + LN_EPS) * g + b


def _split_bf16(v):
    hi = v.astype(BF16)
    lo = (v - hi.astype(F32)).astype(BF16)
    return hi, lo


def _rope(v, cos, sin, dist):
    lane = lax.broadcasted_iota(jnp.int32, v.shape, 1)
    first = (lane % (2 * dist)) < dist
    rot = jnp.where(first, pltpu.roll(v, LANES - dist, 1), pltpu.roll(v, dist, 1))
    return v * cos + rot * sin


def _head_sumsq(v, g_ref):
    hi, lo = _split_bf16(v * v)
    g = g_ref[...]
    return jnp.dot(hi, g, preferred_element_type=F32) + jnp.dot(lo, g, preferred_element_type=F32)


def _inproj_kernel(x_ref, win_ref, wuq_ref, wukv_ref, g512_ref, g128_ref, qg_ref, kg_ref, cqg_ref, ckvg_ref,
                   cos64_ref, sin64_ref, cosm_ref, sinm_ref, q_out, k_out, v_out):
    proj = jnp.dot(x_ref[...].astype(BF16), win_ref[...], preferred_element_type=F32)
    cos64, sin64 = cos64_ref[...], sin64_ref[...]
    cosm, sinm = cosm_ref[...], sinm_ref[...]
    lane = lax.broadcasted_iota(jnp.int32, (proj.shape[0], LANES), 1)
    low = lane < HEAD_DIM

    qg = proj[:, :SZ_Q]
    qn = qg * lax.rsqrt(_head_sumsq(qg, g512_ref) * (1.0 / HEAD_DIM) + RMS_EPS) * qg_ref[...]
    for c in range(SZ_Q // LANES):
        sl = slice(c * LANES, (c + 1) * LANES)
        q_out[:, sl] = _rope(qn[:, sl], cos64, sin64, HEAD_DIM // 4).astype(BF16)

    kg = proj[:, OFF_K:OFF_K + SZ_KV]
    kn = kg * lax.rsqrt(_head_sumsq(kg, g128_ref) * (1.0 / HEAD_DIM) + RMS_EPS) * kg_ref[...]
    kr = _rope(kn, cos64, sin64, HEAD_DIM // 4)
    vg = proj[:, OFF_V:OFF_V + SZ_KV]
    for src, dst in ((kr, k_out), (vg, v_out)):
        lo = jnp.where(low, src, 0.0)
        hi = jnp.where(low, 0.0, src)
        dst[:, 0 * LANES:1 * LANES] = lo.astype(BF16)
        dst[:, 1 * LANES:2 * LANES] = pltpu.roll(lo, HEAD_DIM, 1).astype(BF16)
        dst[:, 2 * LANES:3 * LANES] = pltpu.roll(hi, HEAD_DIM, 1).astype(BF16)
        dst[:, 3 * LANES:4 * LANES] = hi.astype(BF16)

    cq = proj[:, OFF_CQ:OFF_CQ + MLA_Q_RANK]
    cqn = cq * lax.rsqrt(jnp.sum(cq * cq, axis=-1, keepdims=True) * (1.0 / MLA_Q_RANK) + RMS_EPS) * cqg_ref[...]
    qm = jnp.dot(cqn.astype(BF16), wuq_ref[...], preferred_element_type=F32)
    mla_scale = (MLA_NOPE_DIM + MLA_ROPE_DIM) ** -0.5
    for h in range(MLA_HEADS):
        sl = slice(h * LANES, (h + 1) * LANES)
        dsl = slice(SZ_Q + h * LANES, SZ_Q + (h + 1) * LANES)
        q_out[:, dsl] = (_rope(qm[:, sl], cosm, sinm, MLA_ROPE_DIM // 4) * mla_scale).astype(BF16)

    ckv = proj[:, OFF_CKV:OFF_CKV + MLA_KV_RANK]
    ckvn = ckv * lax.rsqrt(jnp.sum(ckv * ckv, axis=-1, keepdims=True) * (1.0 / MLA_KV_RANK) + RMS_EPS) * ckvg_ref[...]
    kv = jnp.dot(ckvn.astype(BF16), wukv_ref[...], preferred_element_type=F32)
    kpe = _rope(proj[:, OFF_KR:OFF_KR + LANES], cosm, sinm, MLA_ROPE_DIM // 4)
    for h in range(MLA_HEADS):
        sl = slice(h * LANES, (h + 1) * LANES)
        dsl = slice(SZ_Q + h * LANES, SZ_Q + (h + 1) * LANES)
        k_out[:, dsl] = (kv[:, sl] + kpe).astype(BF16)
    voff = MLA_HEADS * LANES
    for j in range(MLA_HEADS // 2):
        vp = kv[:, voff + j * LANES: voff + (j + 1) * LANES]
        base = SZ_Q + 2 * j * LANES
        v_out[:, base:base + LANES] = jnp.where(low, vp, 0.0).astype(BF16)
        v_out[:, base + LANES:base + 2 * LANES] = jnp.where(low, 0.0, vp).astype(BF16)


def _inproj_call(x2d, win, wuq, wukv, g512, g128, qg, kg, cqg, ckvg, cos64, sin64, cosm, sinm, seq):
    T = x2d.shape[0]
    tm = TM_IN
    tiles_per_seq = seq // tm
    full = lambda shape: pl.BlockSpec(shape, lambda i: (0,) * len(shape))
    tab = pl.BlockSpec((tm, LANES), lambda i: (i % tiles_per_seq, 0))
    out_spec = pl.BlockSpec((tm, QKV_COLS), lambda i: (i, 0))
    out_shape = jax.ShapeDtypeStruct((T, QKV_COLS), BF16)
    return pl.pallas_call(
        _inproj_kernel,
        grid=(T // tm,),
        in_specs=[pl.BlockSpec((tm, D_MODEL), lambda i: (i, 0)),
                  full(win.shape), full(wuq.shape), full(wukv.shape), full(g512.shape), full(g128.shape),
                  full(qg.shape), full(kg.shape), full(cqg.shape), full(ckvg.shape),
                  tab, tab, tab, tab],
        out_specs=[out_spec, out_spec, out_spec],
        out_shape=[out_shape, out_shape, out_shape],
        compiler_params=_cparams(1),
        name="inproj",
    )(x2d, win, wuq, wukv, g512, g128, qg, kg, cqg, ckvg, cos64, sin64, cosm, sinm)


def _attn_kernel(qe_ref, qo_ref, ke_ref, ko_ref, v_ref, o_ref):
    seq = o_ref.shape[0]

    def head(q, k_ref):
        s = lax.dot_general(q, k_ref[...], (((1,), (1,)), ((), ())), preferred_element_type=F32)
        m = jnp.max(s, axis=-1, keepdims=True)
        p = jnp.exp(s - m)
        l = jnp.sum(p, axis=-1, keepdims=True)
        o = jnp.dot(p.astype(BF16), v_ref[...], preferred_element_type=F32)
        return o / l

    def body(i, carry):
        r0 = pl.multiple_of(i * TQ, TQ)
        oe = head(qe_ref[pl.ds(r0, TQ), :], ke_ref)
        oo = head(qo_ref[pl.ds(r0, TQ), :], ko_ref)
        o_ref[pl.ds(r0, TQ), :] = (oe[:, :LANES] + oo[:, LANES:]).astype(BF16)
        return carry

    lax.fori_loop(0, seq // TQ, body, 0, unroll=True)


def _q_chunk(pr, odd):
    return jnp.where(pr < 4, pr, 4 + 2 * (pr - 4) + odd)


def _kv_chunk(pr, odd):
    return jnp.where(pr < 4, 2 * (pr // 2) + odd, 4 + 2 * (pr - 4) + odd)


def _attn_call(q_all, k_all, v_all, batch, seq):
    T = q_all.shape[0]
    blk = (seq, LANES)
    qspec = lambda odd: pl.BlockSpec(blk, lambda b, pr: (b, _q_chunk(pr, odd)))
    kspec = lambda odd: pl.BlockSpec(blk, lambda b, pr: (b, _kv_chunk(pr, odd)))
    vspec = pl.BlockSpec((seq, 2 * LANES), lambda b, pr: (b, _kv_chunk(pr, 0) // 2))
    return pl.pallas_call(
        _attn_kernel,
        grid=(batch, N_PAIRS),
        in_specs=[qspec(0), qspec(1), kspec(0), kspec(1), vspec],
        out_specs=pl.BlockSpec(blk, lambda b, pr: (b, pr)),
        out_shape=jax.ShapeDtypeStruct((T, MIX_WIDTH), BF16),
        compiler_params=_cparams(2),
        name="attention",
    )(q_all, q_all, k_all, k_all, v_all)


def _outproj_kernel(o_ref, x_ref, wout_ref, g_ref, b_ref, wrh_ref, wrl_ref, x1_ref, x1b_ref, aff_ref):
    mix = jnp.dot(o_ref[...], wout_ref[...], preferred_element_type=F32)
    x1 = _layer_norm(DN_ALPHA * x_ref[...] + mix, g_ref[...], b_ref[...])
    x1_ref[...] = x1
    hi, lo = _split_bf16(x1)
    x1b_ref[...] = hi
    nt = (((1,), (1,)), ((), ()))
    wrh, wrl = wrh_ref[...], wrl_ref[...]
    logits = (lax.dot_general(wrh, hi, nt, preferred_element_type=F32)
              + lax.dot_general(wrh, lo, nt, preferred_element_type=F32)
              + lax.dot_general(wrl, hi, nt, preferred_element_type=F32))
    e = jnp.exp(logits - jnp.max(logits, axis=0, keepdims=True))
    aff_ref[...] = e / jnp.sum(e, axis=0, keepdims=True)


def _outproj_call(o_mix, x2d, wout, g, b, wrh, wrl):
    T = x2d.shape[0]
    tm = TM_OUT
    full = lambda shape: pl.BlockSpec(shape, lambda i: (0,) * len(shape))
    row = lambda w: pl.BlockSpec((tm, w), lambda i: (i, 0))
    return pl.pallas_call(
        _outproj_kernel,
        grid=(T // tm,),
        in_specs=[row(MIX_WIDTH), row(D_MODEL), full(wout.shape), full(g.shape), full(b.shape),
                  full(wrh.shape), full(wrl.shape)],
        out_specs=[row(D_MODEL), row(D_MODEL), pl.BlockSpec((N_EXPERTS, tm), lambda i: (0, i))],
        out_shape=[jax.ShapeDtypeStruct((T, D_MODEL), F32), jax.ShapeDtypeStruct((T, D_MODEL), BF16),
                   jax.ShapeDtypeStruct((N_EXPERTS, T), F32)],
        compiler_params=_cparams(1),
        name="outproj_router",
    )(o_mix, x2d, wout, g, b, wrh, wrl)


def _excl_cumsum(mask, upper):
    n = mask.shape[1]
    ones = jnp.where(mask, 1.0, 0.0)
    run = jnp.zeros((mask.shape[0], 1), F32)
    outs = []
    for j in range(n // LANES):
        blk = ones[:, j * LANES:(j + 1) * LANES]
        outs.append(jnp.dot(blk.astype(BF16), upper, preferred_element_type=F32) + run)
        run = run + jnp.sum(blk, axis=1, keepdims=True)
    return jnp.concatenate(outs, axis=1)


def _topk_kernel(cap, aff_ref, pos_ref, post_ref):
    a = aff_ref[...]
    bits = lax.bitcast_convert_type(a, jnp.int32)
    n_e, seq = a.shape

    def step(k, t):
        cand = t | jnp.left_shift(jnp.int32(1), 30 - k)
        cnt = jnp.sum(jnp.where(bits >= cand, 1.0, 0.0), axis=1, keepdims=True)
        return jnp.where(cnt >= cap, cand, t)

    t = lax.fori_loop(0, 31, step, jnp.zeros((n_e, 1), jnp.int32))
    gt = bits > t
    eq = bits == t
    need = cap - jnp.sum(jnp.where(gt, 1.0, 0.0), axis=1, keepdims=True)
    ri = lax.broadcasted_iota(jnp.int32, (LANES, LANES), 0)
    ci = lax.broadcasted_iota(jnp.int32, (LANES, LANES), 1)
    upper = jnp.where(ri < ci, 1.0, 0.0).astype(BF16)
    sel = gt | (eq & (_excl_cumsum(eq, upper) < need))
    pos = jnp.where(sel, _excl_cumsum(sel, upper), -1.0)
    pos_ref[0] = pos
    padded = jnp.concatenate([pos, jnp.full((LANES - n_e, seq), -1.0, F32)], axis=0)
    post_ref[0] = padded.T


def _topk_call(aff_t, batch, seq, cap):
    return pl.pallas_call(
        functools.partial(_topk_kernel, cap),
        grid=(batch,),
        in_specs=[pl.BlockSpec((N_EXPERTS, seq), lambda b: (0, b))],
        out_specs=[pl.BlockSpec((1, N_EXPERTS, seq), lambda b: (b, 0, 0)),
                   pl.BlockSpec((1, seq, LANES), lambda b: (b, 0, 0))],
        out_shape=[jax.ShapeDtypeStruct((batch, N_EXPERTS, seq), F32),
                   jax.ShapeDtypeStruct((batch, seq, LANES), F32)],
        compiler_params=_cparams(1),
        name="expert_choice",
    )(aff_t)


def _gather_kernel(cap, pos_ref, aff_ref, x_ref, xg_ref, gs_ref):
    posr = pos_ref[0, 0]
    seq = posr.shape[1]
    slot = lax.broadcasted_iota(jnp.int32, (cap, seq), 0).astype(F32)
    onehot = posr == slot
    xg = jnp.dot(jnp.where(onehot, 1.0, 0.0).astype(BF16), x_ref[...], preferred_element_type=F32)
    xg_ref[0] = xg.astype(BF16)
    g = jnp.sum(jnp.where(onehot, aff_ref[0, 0], 0.0), axis=1, keepdims=True)
    gs_ref[0] = jnp.broadcast_to(g, (cap, LANES))


def _gather_call(pos4, aff4, x1b, batch, seq, cap):
    return pl.pallas_call(
        functools.partial(_gather_kernel, cap),
        grid=(batch, N_EXPERTS),
        in_specs=[pl.BlockSpec((1, 1, 1, seq), lambda b, e: (b, e, 0, 0)),
                  pl.BlockSpec((1, 1, 1, seq), lambda b, e: (e, b, 0, 0)),
                  pl.BlockSpec((seq, D_MODEL), lambda b, e: (b, 0))],
        out_specs=[pl.BlockSpec((1, cap, D_MODEL), lambda b, e: (e, b, 0)),
                   pl.BlockSpec((1, cap, LANES), lambda b, e: (e, b, 0))],
        out_shape=[jax.ShapeDtypeStruct((N_EXPERTS, batch * cap, D_MODEL), BF16),
                   jax.ShapeDtypeStruct((N_EXPERTS, batch * cap, LANES), F32)],
        compiler_params=_cparams(2),
        name="moe_gather",
    )(pos4, aff4, x1b)


def _ffn_kernel(xg_ref, gs_ref, wg_ref, wu_ref, wd_ref, y_ref, acc_ref, wgb_ref, wub_ref, wdb_ref):
    f = pl.program_id(2)
    wgb_ref[...] = wg_ref[0].astype(BF16)
    wub_ref[...] = wu_ref[0].astype(BF16)
    wdb_ref[...] = wd_ref[0].astype(BF16)

    @pl.when(f == 0)
    def _():
        acc_ref[...] = jnp.zeros_like(acc_ref)

    def body(i, carry):
        r0 = pl.multiple_of(i * FFN_RB, FFN_RB)
        xb = xg_ref[0, pl.ds(r0, FFN_RB), :]
        g = jnp.dot(xb, wgb_ref[...], preferred_element_type=F32)
        u = jnp.dot(xb, wub_ref[...], preferred_element_type=F32)
        h = (g * jax.nn.sigmoid(g) * u).astype(BF16)
        acc_ref[pl.ds(r0, FFN_RB), :] += jnp.dot(h, wdb_ref[...], preferred_element_type=F32)
        return carry

    lax.fori_loop(0, FFN_ROWS // FFN_RB, body, 0, unroll=True)

    @pl.when(f == pl.num_programs(2) - 1)
    def _():
        y_ref[0] = (acc_ref[...] * gs_ref[0][:, :1]).astype(BF16)


def _ffn_call(xg, gs, w_gate, w_up, w_down):
    n_e, rows, _ = xg.shape
    return pl.pallas_call(
        _ffn_kernel,
        grid=(n_e, rows // FFN_ROWS, EXPERT_FF // FF_TILE),
        in_specs=[pl.BlockSpec((1, FFN_ROWS, D_MODEL), lambda e, m, f: (e, m, 0)),
                  pl.BlockSpec((1, FFN_ROWS, LANES), lambda e, m, f: (e, m, 0)),
                  pl.BlockSpec((1, D_MODEL, FF_TILE), lambda e, m, f: (e, 0, f)),
                  pl.BlockSpec((1, D_MODEL, FF_TILE), lambda e, m, f: (e, 0, f)),
                  pl.BlockSpec((1, FF_TILE, D_MODEL), lambda e, m, f: (e, f, 0))],
        out_specs=pl.BlockSpec((1, FFN_ROWS, D_MODEL), lambda e, m, f: (e, m, 0)),
        out_shape=jax.ShapeDtypeStruct((n_e, rows, D_MODEL), BF16),
        scratch_shapes=[pltpu.VMEM((FFN_ROWS, D_MODEL), F32),
                        pltpu.VMEM((D_MODEL, FF_TILE), BF16),
                        pltpu.VMEM((D_MODEL, FF_TILE), BF16),
                        pltpu.VMEM((FF_TILE, D_MODEL), BF16)],
        compiler_params=_cparams(3),
        name="moe_ffn",
    )(xg, gs, w_gate, w_up, w_down)


def _final_kernel(cap, post_ref, yg_ref, x1_ref, p_ref, wpp_ref, wpg_ref, g2_ref, b2_ref, g3_ref, b3_ref, out_ref):
    post = post_ref[0]
    ts = post.shape[0]
    slot = lax.broadcasted_iota(jnp.int32, (ts, cap), 1).astype(F32)
    moe = jnp.zeros((ts, D_MODEL), F32)
    for e in range(N_EXPERTS):
        onehot = jnp.where(post[:, e:e + 1] == slot, 1.0, 0.0).astype(BF16)
        moe = moe + jnp.dot(onehot, yg_ref[e], preferred_element_type=F32)
    x2 = _layer_norm(DN_ALPHA * x1_ref[...] + moe, g2_ref[...], b2_ref[...])
    emb = jnp.dot(p_ref[...].astype(BF16), wpp_ref[...], preferred_element_type=F32)
    gate = jax.nn.sigmoid(jnp.dot(x2.astype(BF16), wpg_ref[...], preferred_element_type=F32))
    out_ref[...] = _layer_norm(DN_ALPHA * x2 + gate * emb, g3_ref[...], b3_ref[...])


def _final_call(post, yg, x1, p2d, wpp, wpg, g2, b2, g3, b3, batch, seq, cap):
    T = x1.shape[0]
    ts = TS_FIN
    nt = seq // ts
    full = lambda shape: pl.BlockSpec(shape, lambda b, i: (0,) * len(shape))
    row = lambda w: pl.BlockSpec((ts, w), lambda b, i: (b * nt + i, 0))
    return pl.pallas_call(
        functools.partial(_final_kernel, cap),
        grid=(batch, nt),
        in_specs=[pl.BlockSpec((1, ts, LANES), lambda b, i: (b, i, 0)),
                  pl.BlockSpec((N_EXPERTS, cap, D_MODEL), lambda b, i: (0, b, 0)),
                  row(D_MODEL), row(PLE_DIM), full(wpp.shape), full(wpg.shape),
                  full(g2.shape), full(b2.shape), full(g3.shape), full(b3.shape)],
        out_specs=row(D_MODEL),
        out_shape=jax.ShapeDtypeStruct((T, D_MODEL), F32),
        compiler_params=_cparams(2),
        name="combine_ple",
    )(post, yg, x1, p2d, wpp, wpg, g2, b2, g3, b3)


def _rope_tables(seq):
    rows = seq // GRID_W
    row = jnp.repeat(jnp.arange(rows, dtype=F32), GRID_W)
    col = jnp.tile(jnp.arange(GRID_W, dtype=F32), rows)

    def table(head_dim, lane_of):
        sub = head_dim // 2
        half = sub // 2
        inv_freq = ROPE_BASE ** (-jnp.arange(half, dtype=F32) * 2.0 / sub)
        ang_row = row[:, None] * inv_freq[None, :]
        ang_col = col[:, None] * inv_freq[None, :]
        ang = jnp.concatenate([ang_row, ang_col], axis=1)
        idx = np.where(lane_of < 0, 0, (lane_of // sub) * half + (lane_of % sub) % half)
        first = (lane_of % sub) < half
        active = lane_of >= 0
        a = ang[:, idx]
        cos = jnp.where(active[None, :], jnp.cos(a), 1.0)
        sin = jnp.where(active[None, :], jnp.where(first[None, :], -jnp.sin(a), jnp.sin(a)), 0.0)
        return cos, sin

    lanes = np.arange(LANES)
    cos64, sin64 = table(HEAD_DIM, lanes % HEAD_DIM)
    mla_lane = np.where((lanes >= MLA_NOPE_DIM) & (lanes < MLA_NOPE_DIM + MLA_ROPE_DIM), lanes - MLA_NOPE_DIM, -1)
    cosm, sinm = table(MLA_ROPE_DIM, mla_lane)
    return cos64, sin64, cosm, sinm


def _block_ones(n, blk):
    i = np.arange(n) // blk
    return jnp.asarray((i[:, None] == i[None, :]).astype(np.float32), dtype=BF16)


def kernel(x, p, w_in, q_norm, k_norm, cq_norm, ckv_norm, w_uq, w_ukv, w_out, ln_attn_g, ln_attn_b, w_router,
           w_gate, w_up, w_down, ln_ffn_g, ln_ffn_b, w_ple_proj, w_ple_gate, ln_ple_g, ln_ple_b):
    assert w_in.shape[0] == DEPTH
    B, S, D = x.shape
    T = B * S
    cap = CAPACITY_FACTOR * S // N_EXPERTS
    x2d = x.reshape(T, D)

    wi = w_in[0]
    zc = lambda n: jnp.zeros((D, n), F32)
    win = jnp.concatenate([wi[:, :OFF_KR], zc(MLA_NOPE_DIM), wi[:, OFF_KR:], zc(LANES - MLA_NOPE_DIM - MLA_ROPE_DIM)],
                          axis=1).astype(BF16)
    qk = MLA_NOPE_DIM + MLA_ROPE_DIM
    wuq = jnp.pad(w_uq[0].reshape(MLA_Q_RANK, MLA_HEADS, qk), ((0, 0), (0, 0), (0, LANES - qk)))
    wuq = wuq.reshape(MLA_Q_RANK, MLA_HEADS * LANES).astype(BF16)
    wkv = w_ukv[0].reshape(MLA_KV_RANK, MLA_HEADS, MLA_NOPE_DIM + MLA_V_DIM)
    wk = jnp.pad(wkv[:, :, :MLA_NOPE_DIM], ((0, 0), (0, 0), (0, LANES - MLA_NOPE_DIM)))
    wukv = jnp.concatenate([wk.reshape(MLA_KV_RANK, MLA_HEADS * LANES),
                            wkv[:, :, MLA_NOPE_DIM:].reshape(MLA_KV_RANK, MLA_HEADS * MLA_V_DIM)], axis=1).astype(BF16)
    qg = (jnp.tile(q_norm[0], GQA_HEADS) * (HEAD_DIM ** -0.5)).reshape(1, SZ_Q)
    kg = jnp.tile(k_norm[0], GQA_KV_HEADS).reshape(1, SZ_KV)
    cqg = cq_norm[0].reshape(1, MLA_Q_RANK)
    ckvg = ckv_norm[0].reshape(1, MLA_KV_RANK)
    cos64, sin64, cosm, sinm = _rope_tables(S)

    q_all, k_all, v_all = _inproj_call(x2d, win, wuq, wukv, _block_ones(SZ_Q, HEAD_DIM), _block_ones(SZ_KV, HEAD_DIM),
                                       qg, kg, cqg, ckvg, cos64, sin64, cosm, sinm, S)
    o_mix = _attn_call(q_all, k_all, v_all, B, S)

    wr_t = w_router[0].T
    wrh = wr_t.astype(BF16)
    wrl = (wr_t - wrh.astype(F32)).astype(BF16)
    x1, x1b, aff_t = _outproj_call(o_mix, x2d, w_out[0].astype(BF16), ln_attn_g[0].reshape(1, D),
                                   ln_attn_b[0].reshape(1, D), wrh, wrl)

    pos, post = _topk_call(aff_t, B, S, cap)
    xg, gs = _gather_call(pos.reshape(B, N_EXPERTS, 1, S), aff_t.reshape(N_EXPERTS, B, 1, S), x1b, B, S, cap)
    yg = _ffn_call(xg, gs, w_gate[0], w_up[0], w_down[0])
    out = _final_call(post, yg, x1, p[0].reshape(T, PLE_DIM), w_ple_proj[0].astype(BF16), w_ple_gate[0].astype(BF16),
                      ln_ffn_g[0].reshape(1, D), ln_ffn_b[0].reshape(1, D), ln_ple_g[0].reshape(1, D),
                      ln_ple_b[0].reshape(1, D), B, S, cap)
    return out.reshape(B, S, D)
```

```python
import functools

import numpy as np
import jax
import jax.numpy as jnp
from jax import lax
from jax.experimental import pallas as pl
from jax.experimental.pallas import tpu as pltpu

F32 = jnp.float32
BF16 = jnp.bfloat16

D_MODEL = 1024
HEAD_DIM = 64
GQA_HEADS = 8
GQA_KV_HEADS = 2
MLA_HEADS = 8
MLA_NOPE_DIM = 64
MLA_ROPE_DIM = 32
MLA_V_DIM = 64
MLA_Q_RANK = 256
MLA_KV_RANK = 256
N_EXPERTS = 16
CAPACITY_FACTOR = 2
EXPERT_FF = 2816
PLE_DIM = 256
GRID_W = 64
ROPE_BASE = 10000.0
LN_EPS = 1e-5
RMS_EPS = 1e-6
DEPTH = 1
DN_ALPHA = (2 * DEPTH) ** 0.25

LANES = 128
SZ_Q = GQA_HEADS * HEAD_DIM
SZ_KV = GQA_KV_HEADS * HEAD_DIM
IN_COLS = SZ_Q + 2 * SZ_KV + MLA_Q_RANK + MLA_KV_RANK + LANES
OFF_K = SZ_Q
OFF_V = OFF_K + SZ_KV
OFF_CQ = OFF_V + SZ_KV
OFF_CKV = OFF_CQ + MLA_Q_RANK
OFF_KR = OFF_CKV + MLA_KV_RANK
N_CHUNKS = 12
N_PAIRS = 8
QKV_COLS = N_CHUNKS * LANES
MIX_WIDTH = N_PAIRS * LANES

TM_IN = 1024
TQ = 256
TM_OUT = 1024
FF_TILE = 256
FFN_ROWS = 2048
FFN_RB = 512
TS_FIN = 512
TOK_BLK = 256
WIN = 64
WIN_ALIGN = 16
TAB_COLS = 16
TAB_FLAG = 8

VMEM_LIMIT = 56 * 1024 * 1024


def _cparams(n_axes, vmem=VMEM_LIMIT):
    return pltpu.CompilerParams(dimension_semantics=("arbitrary",) * n_axes, vmem_limit_bytes=vmem)


def _layer_norm(y, g, b):
    mu = jnp.mean(y, axis=-1, keepdims=True)
    yc = y - mu
    var = jnp.mean(yc * yc, axis=-1, keepdims=True)
    return yc * lax.rsqrt(var + LN_EPS) * g + b


def _split_bf16(v):
    hi = v.astype(BF16)
    lo = (v - hi.astype(F32)).astype(BF16)
    return hi, lo


def _rope(v, cos, sin, dist):
    lane = lax.broadcasted_iota(jnp.int32, v.shape, 1)
    first = (lane % (2 * dist)) < dist
    rot = jnp.where(first, pltpu.roll(v, LANES - dist, 1), pltpu.roll(v, dist, 1))
    return v * cos + rot * sin


def _head_sumsq(v, g_ref):
    hi, lo = _split_bf16(v * v)
    g = g_ref[...]
    return jnp.dot(hi, g, preferred_element_type=F32) + jnp.dot(lo, g, preferred_element_type=F32)


def _inproj_kernel(x_ref, win_ref, wuq_ref, wukv_ref, g512_ref, g128_ref, qg_ref, kg_ref, cqg_ref, ckvg_ref,
                   cos64_ref, sin64_ref, cosm_ref, sinm_ref, q_out, k_out, v_out):
    proj = jnp.dot(x_ref[...].astype(BF16), win_ref[...], preferred_element_type=F32)
    cos64, sin64 = cos64_ref[...], sin64_ref[...]
    cosm, sinm = cosm_ref[...], sinm_ref[...]
    lane = lax.broadcasted_iota(jnp.int32, (proj.shape[0], LANES), 1)
    low = lane < HEAD_DIM

    qg = proj[:, :SZ_Q]
    qn = qg * lax.rsqrt(_head_sumsq(qg, g512_ref) * (1.0 / HEAD_DIM) + RMS_EPS) * qg_ref[...]
    for c in range(SZ_Q // LANES):
        sl = slice(c * LANES, (c + 1) * LANES)
        q_out[:, sl] = _rope(qn[:, sl], cos64, sin64, HEAD_DIM // 4).astype(BF16)

    kg = proj[:, OFF_K:OFF_K + SZ_KV]
    kn = kg * lax.rsqrt(_head_sumsq(kg, g128_ref) * (1.0 / HEAD_DIM) + RMS_EPS) * kg_ref[...]
    kr = _rope(kn, cos64, sin64, HEAD_DIM // 4)
    vg = proj[:, OFF_V:OFF_V + SZ_KV]
    for src, dst in ((kr, k_out), (vg, v_out)):
        lo = jnp.where(low, src, 0.0)
        hi = jnp.where(low, 0.0, src)
        dst[:, 0 * LANES:1 * LANES] = lo.astype(BF16)
        dst[:, 1 * LANES:2 * LANES] = pltpu.roll(lo, HEAD_DIM, 1).astype(BF16)
        dst[:, 2 * LANES:3 * LANES] = pltpu.roll(hi, HEAD_DIM, 1).astype(BF16)
        dst[:, 3 * LANES:4 * LANES] = hi.astype(BF16)

    cq = proj[:, OFF_CQ:OFF_CQ + MLA_Q_RANK]
    cqn = cq * lax.rsqrt(jnp.sum(cq * cq, axis=-1, keepdims=True) * (1.0 / MLA_Q_RANK) + RMS_EPS) * cqg_ref[...]
    qm = jnp.dot(cqn.astype(BF16), wuq_ref[...], preferred_element_type=F32)
    mla_scale = (MLA_NOPE_DIM + MLA_ROPE_DIM) ** -0.5
    for h in range(MLA_HEADS):
        sl = slice(h * LANES, (h + 1) * LANES)
        dsl = slice(SZ_Q + h * LANES, SZ_Q + (h + 1) * LANES)
        q_out[:, dsl] = (_rope(qm[:, sl], cosm, sinm, MLA_ROPE_DIM // 4) * mla_scale).astype(BF16)

    ckv = proj[:, OFF_CKV:OFF_CKV + MLA_KV_RANK]
    ckvn = ckv * lax.rsqrt(jnp.sum(ckv * ckv, axis=-1, keepdims=True) * (1.0 / MLA_KV_RANK) + RMS_EPS) * ckvg_ref[...]
    kv = jnp.dot(ckvn.astype(BF16), wukv_ref[...], preferred_element_type=F32)
    kpe = _rope(proj[:, OFF_KR:OFF_KR + LANES], cosm, sinm, MLA_ROPE_DIM // 4)
    for h in range(MLA_HEADS):
        sl = slice(h * LANES, (h + 1) * LANES)
        dsl = slice(SZ_Q + h * LANES, SZ_Q + (h + 1) * LANES)
        k_out[:, dsl] = (kv[:, sl] + kpe).astype(BF16)
    voff = MLA_HEADS * LANES
    for j in range(MLA_HEADS // 2):
        vp = kv[:, voff + j * LANES: voff + (j + 1) * LANES]
        base = SZ_Q + 2 * j * LANES
        v_out[:, base:base + LANES] = jnp.where(low, vp, 0.0).astype(BF16)
        v_out[:, base + LANES:base + 2 * LANES] = jnp.where(low, 0.0, vp).astype(BF16)


def _inproj_call(x2d, win, wuq, wukv, g512, g128, qg, kg, cqg, ckvg, cos64, sin64, cosm, sinm, seq):
    T = x2d.shape[0]
    tm = TM_IN
    tiles_per_seq = seq // tm
    full = lambda shape: pl.BlockSpec(shape, lambda i: (0,) * len(shape))
    tab = pl.BlockSpec((tm, LANES), lambda i: (i % tiles_per_seq, 0))
    out_spec = pl.BlockSpec((tm, QKV_COLS), lambda i: (i, 0))
    out_shape = jax.ShapeDtypeStruct((T, QKV_COLS), BF16)
    return pl.pallas_call(
        _inproj_kernel,
        grid=(T // tm,),
        in_specs=[pl.BlockSpec((tm, D_MODEL), lambda i: (i, 0)),
                  full(win.shape), full(wuq.shape), full(wukv.shape), full(g512.shape), full(g128.shape),
                  full(qg.shape), full(kg.shape), full(cqg.shape), full(ckvg.shape),
                  tab, tab, tab, tab],
        out_specs=[out_spec, out_spec, out_spec],
        out_shape=[out_shape, out_shape, out_shape],
        compiler_params=_cparams(1),
        name="inproj",
    )(x2d, win, wuq, wukv, g512, g128, qg, kg, cqg, ckvg, cos64, sin64, cosm, sinm)


def _attn_kernel(qe_ref, qo_ref, ke_ref, ko_ref, v_ref, o_ref):
    seq = o_ref.shape[0]

    def head(q, k_ref):
        s = lax.dot_general(q, k_ref[...], (((1,), (1,)), ((), ())), preferred_element_type=F32)
        m = jnp.max(s, axis=-1, keepdims=True)
        p = jnp.exp(s - m)
        l = jnp.sum(p, axis=-1, keepdims=True)
        o = jnp.dot(p.astype(BF16), v_ref[...], preferred_element_type=F32)
        return o / l

    def body(i, carry):
        r0 = pl.multiple_of(i * TQ, TQ)
        oe = head(qe_ref[pl.ds(r0, TQ), :], ke_ref)
        oo = head(qo_ref[pl.ds(r0, TQ), :], ko_ref)
        o_ref[pl.ds(r0, TQ), :] = (oe[:, :LANES] + oo[:, LANES:]).astype(BF16)
        return carry

    lax.fori_loop(0, seq // TQ, body, 0, unroll=True)


def _q_chunk(pr, odd):
    return jnp.where(pr < 4, pr, 4 + 2 * (pr - 4) + odd)


def _kv_chunk(pr, odd):
    return jnp.where(pr < 4, 2 * (pr // 2) + odd, 4 + 2 * (pr - 4) + odd)


def _attn_call(q_all, k_all, v_all, batch, seq):
    T = q_all.shape[0]
    blk = (seq, LANES)
    qspec = lambda odd: pl.BlockSpec(blk, lambda b, pr: (b, _q_chunk(pr, odd)))
    kspec = lambda odd: pl.BlockSpec(blk, lambda b, pr: (b, _kv_chunk(pr, odd)))
    vspec = pl.BlockSpec((seq, 2 * LANES), lambda b, pr: (b, _kv_chunk(pr, 0) // 2))
    return pl.pallas_call(
        _attn_kernel,
        grid=(batch, N_PAIRS),
        in_specs=[qspec(0), qspec(1), kspec(0), kspec(1), vspec],
        out_specs=pl.BlockSpec(blk, lambda b, pr: (b, pr)),
        out_shape=jax.ShapeDtypeStruct((T, MIX_WIDTH), BF16),
        compiler_params=_cparams(2),
        name="attention",
    )(q_all, q_all, k_all, k_all, v_all)


def _outproj_kernel(o_ref, x_ref, wout_ref, g_ref, b_ref, wrh_ref, wrl_ref, x1_ref, x1b_ref, aff_ref):
    mix = jnp.dot(o_ref[...], wout_ref[...], preferred_element_type=F32)
    x1 = _layer_norm(DN_ALPHA * x_ref[...] + mix, g_ref[...], b_ref[...])
    x1_ref[...] = x1
    hi, lo = _split_bf16(x1)
    x1b_ref[...] = hi
    nt = (((1,), (1,)), ((), ()))
    wrh, wrl = wrh_ref[...], wrl_ref[...]
    logits = (lax.dot_general(wrh, hi, nt, preferred_element_type=F32)
              + lax.dot_general(wrh, lo, nt, preferred_element_type=F32)
              + lax.dot_general(wrl, hi, nt, preferred_element_type=F32))
    e = jnp.exp(logits - jnp.max(logits, axis=0, keepdims=True))
    aff_ref[...] = e / jnp.sum(e, axis=0, keepdims=True)


def _outproj_call(o_mix, x2d, wout, g, b, wrh, wrl):
    T = x2d.shape[0]
    tm = TM_OUT
    full = lambda shape: pl.BlockSpec(shape, lambda i: (0,) * len(shape))
    row = lambda w: pl.BlockSpec((tm, w), lambda i: (i, 0))
    return pl.pallas_call(
        _outproj_kernel,
        grid=(T // tm,),
        in_specs=[row(MIX_WIDTH), row(D_MODEL), full(wout.shape), full(g.shape), full(b.shape),
                  full(wrh.shape), full(wrl.shape)],
        out_specs=[row(D_MODEL), row(D_MODEL), pl.BlockSpec((N_EXPERTS, tm), lambda i: (0, i))],
        out_shape=[jax.ShapeDtypeStruct((T, D_MODEL), F32), jax.ShapeDtypeStruct((T, D_MODEL), BF16),
                   jax.ShapeDtypeStruct((N_EXPERTS, T), F32)],
        compiler_params=_cparams(1),
        name="outproj_router",
    )(o_mix, x2d, wout, g, b, wrh, wrl)


def _excl_cumsum(mask, upper):
    n = mask.shape[1]
    ones = jnp.where(mask, 1.0, 0.0)
    run = jnp.zeros((mask.shape[0], 1), F32)
    outs = []
    for j in range(n // LANES):
        blk = ones[:, j * LANES:(j + 1) * LANES]
        outs.append(jnp.dot(blk.astype(BF16), upper, preferred_element_type=F32) + run)
        run = run + jnp.sum(blk, axis=1, keepdims=True)
    return jnp.concatenate(outs, axis=1)


def _topk_kernel(cap, aff_ref, pos_ref, post_ref, tab_ref):
    a = aff_ref[...]
    bits = lax.bitcast_convert_type(a, jnp.int32)
    n_e, seq = a.shape

    def step(k, t):
        cand = t | jnp.left_shift(jnp.int32(1), 30 - k)
        cnt = jnp.sum(jnp.where(bits >= cand, 1.0, 0.0), axis=1, keepdims=True)
        return jnp.where(cnt >= cap, cand, t)

    t = lax.fori_loop(0, 31, step, jnp.zeros((n_e, 1), jnp.int32))
    gt = bits > t
    eq = bits == t
    need = cap - jnp.sum(jnp.where(gt, 1.0, 0.0), axis=1, keepdims=True)
    ri = lax.broadcasted_iota(jnp.int32, (LANES, LANES), 0)
    ci = lax.broadcasted_iota(jnp.int32, (LANES, LANES), 1)
    upper = jnp.where(ri < ci, 1.0, 0.0).astype(BF16)
    sel = gt | (eq & (_excl_cumsum(eq, upper) < need))
    pos = jnp.where(sel, _excl_cumsum(sel, upper), -1.0)
    pos_ref[0] = pos
    padded = jnp.concatenate([pos, jnp.full((LANES - n_e, seq), -1.0, F32)], axis=0)
    post_ref[0] = padded.T

    lane = lax.broadcasted_iota(jnp.int32, (n_e, LANES), 1)
    ones = jnp.where(sel, 1.0, 0.0)
    lo = jnp.zeros((n_e, 1), F32)
    tab = jnp.zeros((n_e, LANES), F32)
    over = jnp.zeros((n_e, 1), F32)
    for kb in range(seq // TOK_BLK):
        hi = lo + jnp.sum(ones[:, kb * TOK_BLK:(kb + 1) * TOK_BLK], axis=1, keepdims=True)
        win = jnp.minimum(jnp.floor(lo * (1.0 / WIN_ALIGN)) * WIN_ALIGN, float(cap - WIN))
        over = jnp.maximum(over, jnp.where(hi - win > WIN, 1.0, 0.0))
        tab = jnp.where(lane == kb, win, tab)
        lo = hi
    tab = jnp.where(lane == TAB_FLAG, jnp.max(over, axis=0, keepdims=True), tab)
    tab_ref[0] = tab.astype(jnp.int32)


def _topk_call(aff_t, batch, seq, cap):
    assert seq // TOK_BLK <= TAB_FLAG < TAB_COLS and cap % WIN == 0 and WIN % WIN_ALIGN == 0
    return pl.pallas_call(
        functools.partial(_topk_kernel, cap),
        grid=(batch,),
        in_specs=[pl.BlockSpec((N_EXPERTS, seq), lambda b: (0, b))],
        out_specs=[pl.BlockSpec((1, N_EXPERTS, seq), lambda b: (b, 0, 0)),
                   pl.BlockSpec((1, seq, LANES), lambda b: (b, 0, 0)),
                   pl.BlockSpec((1, N_EXPERTS, LANES), lambda b: (b, 0, 0))],
        out_shape=[jax.ShapeDtypeStruct((batch, N_EXPERTS, seq), F32),
                   jax.ShapeDtypeStruct((batch, seq, LANES), F32),
                   jax.ShapeDtypeStruct((batch, N_EXPERTS, LANES), jnp.int32)],
        compiler_params=_cparams(1),
        name="expert_choice",
    )(aff_t)


def _tab_at(tab_ref, b, e, col):
    return tab_ref[(b * N_EXPERTS + e) * TAB_COLS + col]


def _gather_kernel(cap, tab_ref, pos_ref, aff_ref, x_ref, xg_ref, gs_ref):
    b = pl.program_id(0)
    seq = x_ref.shape[0]
    fits = _tab_at(tab_ref, b, 0, TAB_FLAG) == 0

    @pl.when(fits)
    def _():
        xg_ref[...] = jnp.zeros_like(xg_ref)
        gs_ref[...] = jnp.zeros_like(gs_ref)
        wslot = lax.broadcasted_iota(jnp.int32, (WIN, 1), 0)
        for kb in range(seq // TOK_BLK):
            ts = slice(kb * TOK_BLK, (kb + 1) * TOK_BLK)
            wins, hots = [], []
            for e in range(N_EXPERTS):
                win = pl.multiple_of(_tab_at(tab_ref, b, e, kb), WIN_ALIGN)
                hit = pos_ref[0, e:e + 1, ts] == (wslot + win).astype(F32)
                g = jnp.sum(jnp.where(hit, aff_ref[e:e + 1, ts], 0.0), axis=1, keepdims=True)
                gs_ref[e, pl.ds(win, WIN), :] += jnp.broadcast_to(g, (WIN, LANES))
                wins.append(win)
                hots.append(jnp.where(hit, 1.0, 0.0).astype(BF16))
            rows = jnp.dot(jnp.concatenate(hots, axis=0), x_ref[ts, :], preferred_element_type=F32)
            for e in range(N_EXPERTS):
                xg_ref[e, pl.ds(wins[e], WIN), :] += rows[e * WIN:(e + 1) * WIN].astype(BF16)

    @pl.when(jnp.logical_not(fits))
    def _():
        slot = lax.broadcasted_iota(jnp.int32, (cap, seq), 0).astype(F32)

        def body(e, carry):
            onehot = pos_ref[0, pl.ds(e, 1), :] == slot
            xg = jnp.dot(jnp.where(onehot, 1.0, 0.0).astype(BF16), x_ref[...], preferred_element_type=F32)
            xg_ref[e] = xg.astype(BF16)
            g = jnp.sum(jnp.where(onehot, aff_ref[pl.ds(e, 1), :], 0.0), axis=1, keepdims=True)
            gs_ref[e] = jnp.broadcast_to(g, (cap, LANES))
            return carry

        lax.fori_loop(0, N_EXPERTS, body, 0)


def _gather_call(tab, pos, aff_t, x1b, batch, seq, cap):
    grid_spec = pltpu.PrefetchScalarGridSpec(
        num_scalar_prefetch=1,
        grid=(batch,),
        in_specs=[pl.BlockSpec((1, N_EXPERTS, seq), lambda b, tab: (b, 0, 0)),
                  pl.BlockSpec((N_EXPERTS, seq), lambda b, tab: (0, b)),
                  pl.BlockSpec((seq, D_MODEL), lambda b, tab: (b, 0))],
        out_specs=[pl.BlockSpec((N_EXPERTS, cap, D_MODEL), lambda b, tab: (0, b, 0)),
                   pl.BlockSpec((N_EXPERTS, cap, LANES), lambda b, tab: (0, b, 0))],
    )
    return pl.pallas_call(
        functools.partial(_gather_kernel, cap),
        grid_spec=grid_spec,
        out_shape=[jax.ShapeDtypeStruct((N_EXPERTS, batch * cap, D_MODEL), BF16),
                   jax.ShapeDtypeStruct((N_EXPERTS, batch * cap, LANES), F32)],
        compiler_params=_cparams(1),
        name="moe_gather",
    )(tab, pos, aff_t, x1b)


def _ffn_kernel(xg_ref, gs_ref, wg_ref, wu_ref, wd_ref, y_ref, acc_ref, wgb_ref, wub_ref, wdb_ref):
    f = pl.program_id(2)
    wgb_ref[...] = wg_ref[0].astype(BF16)
    wub_ref[...] = wu_ref[0].astype(BF16)
    wdb_ref[...] = wd_ref[0].astype(BF16)

    @pl.when(f == 0)
    def _():
        acc_ref[...] = jnp.zeros_like(acc_ref)

    def body(i, carry):
        r0 = pl.multiple_of(i * FFN_RB, FFN_RB)
        xb = xg_ref[0, pl.ds(r0, FFN_RB), :]
        g = jnp.dot(xb, wgb_ref[...], preferred_element_type=F32)
        u = jnp.dot(xb, wub_ref[...], preferred_element_type=F32)
        h = (g * jax.nn.sigmoid(g) * u).astype(BF16)
        acc_ref[pl.ds(r0, FFN_RB), :] += jnp.dot(h, wdb_ref[...], preferred_element_type=F32)
        return carry

    lax.fori_loop(0, FFN_ROWS // FFN_RB, body, 0, unroll=True)

    @pl.when(f == pl.num_programs(2) - 1)
    def _():
        y_ref[0] = (acc_ref[...] * gs_ref[0][:, :1]).astype(BF16)


def _ffn_call(xg, gs, w_gate, w_up, w_down):
    n_e, rows, _ = xg.shape
    return pl.pallas_call(
        _ffn_kernel,
        grid=(n_e, rows // FFN_ROWS, EXPERT_FF // FF_TILE),
        in_specs=[pl.BlockSpec((1, FFN_ROWS, D_MODEL), lambda e, m, f: (e, m, 0)),
                  pl.BlockSpec((1, FFN_ROWS, LANES), lambda e, m, f: (e, m, 0)),
                  pl.BlockSpec((1, D_MODEL, FF_TILE), lambda e, m, f: (e, 0, f)),
                  pl.BlockSpec((1, D_MODEL, FF_TILE), lambda e, m, f: (e, 0, f)),
                  pl.BlockSpec((1, FF_TILE, D_MODEL), lambda e, m, f: (e, f, 0))],
        out_specs=pl.BlockSpec((1, FFN_ROWS, D_MODEL), lambda e, m, f: (e, m, 0)),
        out_shape=jax.ShapeDtypeStruct((n_e, rows, D_MODEL), BF16),
        scratch_shapes=[pltpu.VMEM((FFN_ROWS, D_MODEL), F32),
                        pltpu.VMEM((D_MODEL, FF_TILE), BF16),
                        pltpu.VMEM((D_MODEL, FF_TILE), BF16),
                        pltpu.VMEM((FF_TILE, D_MODEL), BF16)],
        compiler_params=_cparams(3),
        name="moe_ffn",
    )(xg, gs, w_gate, w_up, w_down)


def _final_kernel(cap, tab_ref, post_ref, yg_ref, x1_ref, p_ref, wpp_ref, wpg_ref, g2_ref, b2_ref, g3_ref, b3_ref,
                  out_ref, moe_ref):
    b = pl.program_id(0)
    ts = moe_ref.shape[0]
    fits = _tab_at(tab_ref, b, 0, TAB_FLAG) == 0
    per_tile = TOK_BLK // WIN

    @pl.when(fits)
    def _():
        lane = lax.broadcasted_iota(jnp.int32, (1, TOK_BLK), 1)
        grp = lane // WIN
        for j in range(ts // TOK_BLK):
            kb = pl.program_id(1) * (ts // TOK_BLK) + j
            post = post_ref[0, j * TOK_BLK:(j + 1) * TOK_BLK, :]
            acc = jnp.zeros((TOK_BLK, D_MODEL), F32)
            for e0 in range(0, N_EXPERTS, per_tile):
                wins = [pl.multiple_of(_tab_at(tab_ref, b, e0 + r, kb), WIN_ALIGN) for r in range(per_tile)]
                rhs = jnp.concatenate([yg_ref[e0 + r, pl.ds(wins[r], WIN), :] for r in range(per_tile)], axis=0)
                tgt = (lane - grp * WIN).astype(F32)
                col = jnp.zeros((TOK_BLK, TOK_BLK), F32)
                for r in range(per_tile):
                    tgt = tgt + jnp.where(grp == r, wins[r].astype(F32), 0.0)
                    col = jnp.where(grp == r, post[:, e0 + r:e0 + r + 1], col)
                onehot = jnp.where(col == tgt, 1.0, 0.0).astype(BF16)
                acc = acc + jnp.dot(onehot, rhs, preferred_element_type=F32)
            moe_ref[j * TOK_BLK:(j + 1) * TOK_BLK, :] = acc

    @pl.when(jnp.logical_not(fits))
    def _():
        slot = lax.broadcasted_iota(jnp.int32, (ts, cap), 1).astype(F32)

        def body(e, moe):
            lane_e = lax.broadcasted_iota(jnp.int32, (1, LANES), 1) == e
            col = jnp.sum(jnp.where(lane_e, post_ref[0], 0.0), axis=1, keepdims=True)
            onehot = jnp.where(col == slot, 1.0, 0.0).astype(BF16)
            return moe + jnp.dot(onehot, yg_ref[e], preferred_element_type=F32)

        moe_ref[...] = lax.fori_loop(0, N_EXPERTS, body, jnp.zeros((ts, D_MODEL), F32))

    moe = moe_ref[...]
    x2 = _layer_norm(DN_ALPHA * x1_ref[...] + moe, g2_ref[...], b2_ref[...])
    emb = jnp.dot(p_ref[...].astype(BF16), wpp_ref[...], preferred_element_type=F32)
    gate = jax.nn.sigmoid(jnp.dot(x2.astype(BF16), wpg_ref[...], preferred_element_type=F32))
    out_ref[...] = _layer_norm(DN_ALPHA * x2 + gate * emb, g3_ref[...], b3_ref[...])


def _final_call(tab, post, yg, x1, p2d, wpp, wpg, g2, b2, g3, b3, batch, seq, cap):
    T = x1.shape[0]
    ts = TS_FIN
    nt = seq // ts
    full = lambda shape: pl.BlockSpec(shape, lambda b, i, tab: (0,) * len(shape))
    row = lambda w: pl.BlockSpec((ts, w), lambda b, i, tab: (b * nt + i, 0))
    grid_spec = pltpu.PrefetchScalarGridSpec(
        num_scalar_prefetch=1,
        grid=(batch, nt),
        in_specs=[pl.BlockSpec((1, ts, LANES), lambda b, i, tab: (b, i, 0)),
                  pl.BlockSpec((N_EXPERTS, cap, D_MODEL), lambda b, i, tab: (0, b, 0)),
                  row(D_MODEL), row(PLE_DIM), full(wpp.shape), full(wpg.shape),
                  full(g2.shape), full(b2.shape), full(g3.shape), full(b3.shape)],
        out_specs=row(D_MODEL),
        scratch_shapes=[pltpu.VMEM((ts, D_MODEL), F32)],
    )
    return pl.pallas_call(
        functools.partial(_final_kernel, cap),
        grid_spec=grid_spec,
        out_shape=jax.ShapeDtypeStruct((T, D_MODEL), F32),
        compiler_params=_cparams(2),
        name="combine_ple",
    )(tab, post, yg, x1, p2d, wpp, wpg, g2, b2, g3, b3)


def _rope_tables(seq):
    rows = seq // GRID_W
    row = jnp.repeat(jnp.arange(rows, dtype=F32), GRID_W)
    col = jnp.tile(jnp.arange(GRID_W, dtype=F32), rows)

    def table(head_dim, lane_of):
        sub = head_dim // 2
        half = sub // 2
        inv_freq = ROPE_BASE ** (-jnp.arange(half, dtype=F32) * 2.0 / sub)
        ang_row = row[:, None] * inv_freq[None, :]
        ang_col = col[:, None] * inv_freq[None, :]
        ang = jnp.concatenate([ang_row, ang_col], axis=1)
        idx = np.where(lane_of < 0, 0, (lane_of // sub) * half + (lane_of % sub) % half)
        first = (lane_of % sub) < half
        active = lane_of >= 0
        a = ang[:, idx]
        cos = jnp.where(active[None, :], jnp.cos(a), 1.0)
        sin = jnp.where(active[None, :], jnp.where(first[None, :], -jnp.sin(a), jnp.sin(a)), 0.0)
        return cos, sin

    lanes = np.arange(LANES)
    cos64, sin64 = table(HEAD_DIM, lanes % HEAD_DIM)
    mla_lane = np.where((lanes >= MLA_NOPE_DIM) & (lanes < MLA_NOPE_DIM + MLA_ROPE_DIM), lanes - MLA_NOPE_DIM, -1)
    cosm, sinm = table(MLA_ROPE_DIM, mla_lane)
    return cos64, sin64, cosm, sinm


def _block_ones(n, blk):
    i = np.arange(n) // blk
    return jnp.asarray((i[:, None] == i[None, :]).astype(np.float32), dtype=BF16)


def kernel(x, p, w_in, q_norm, k_norm, cq_norm, ckv_norm, w_uq, w_ukv, w_out, ln_attn_g, ln_attn_b, w_router,
           w_gate, w_up, w_down, ln_ffn_g, ln_ffn_b, w_ple_proj, w_ple_gate, ln_ple_g, ln_ple_b):
    assert w_in.shape[0] == DEPTH
    B, S, D = x.shape
    T = B * S
    cap = CAPACITY_FACTOR * S // N_EXPERTS
    x2d = x.reshape(T, D)

    wi = w_in[0]
    zc = lambda n: jnp.zeros((D, n), F32)
    win = jnp.concatenate([wi[:, :OFF_KR], zc(MLA_NOPE_DIM), wi[:, OFF_KR:], zc(LANES - MLA_NOPE_DIM - MLA_ROPE_DIM)],
                          axis=1).astype(BF16)
    qk = MLA_NOPE_DIM + MLA_ROPE_DIM
    wuq = jnp.pad(w_uq[0].reshape(MLA_Q_RANK, MLA_HEADS, qk), ((0, 0), (0, 0), (0, LANES - qk)))
    wuq = wuq.reshape(MLA_Q_RANK, MLA_HEADS * LANES).astype(BF16)
    wkv = w_ukv[0].reshape(MLA_KV_RANK, MLA_HEADS, MLA_NOPE_DIM + MLA_V_DIM)
    wk = jnp.pad(wkv[:, :, :MLA_NOPE_DIM], ((0, 0), (0, 0), (0, LANES - MLA_NOPE_DIM)))
    wukv = jnp.concatenate([wk.reshape(MLA_KV_RANK, MLA_HEADS * LANES),
                            wkv[:, :, MLA_NOPE_DIM:].reshape(MLA_KV_RANK, MLA_HEADS * MLA_V_DIM)], axis=1).astype(BF16)
    qg = (jnp.tile(q_norm[0], GQA_HEADS) * (HEAD_DIM ** -0.5)).reshape(1, SZ_Q)
    kg = jnp.tile(k_norm[0], GQA_KV_HEADS).reshape(1, SZ_KV)
    cqg = cq_norm[0].reshape(1, MLA_Q_RANK)
    ckvg = ckv_norm[0].reshape(1, MLA_KV_RANK)
    cos64, sin64, cosm, sinm = _rope_tables(S)

    q_all, k_all, v_all = _inproj_call(x2d, win, wuq, wukv, _block_ones(SZ_Q, HEAD_DIM), _block_ones(SZ_KV, HEAD_DIM),
                                       qg, kg, cqg, ckvg, cos64, sin64, cosm, sinm, S)
    o_mix = _attn_call(q_all, k_all, v_all, B, S)

    wr_t = w_router[0].T
    wrh = wr_t.astype(BF16)
    wrl = (wr_t - wrh.astype(F32)).astype(BF16)
    x1, x1b, aff_t = _outproj_call(o_mix, x2d, w_out[0].astype(BF16), ln_attn_g[0].reshape(1, D),
                                   ln_attn_b[0].reshape(1, D), wrh, wrl)

    pos, post, tab = _topk_call(aff_t, B, S, cap)
    tab = tab[:, :, :TAB_COLS].reshape(-1)
    xg, gs = _gather_call(tab, pos, aff_t, x1b, B, S, cap)
    yg = _ffn_call(xg, gs, w_gate[0], w_up[0], w_down[0])
    out = _final_call(tab, post, yg, x1, p[0].reshape(T, PLE_DIM), w_ple_proj[0].astype(BF16), w_ple_gate[0].astype(BF16),
                      ln_ffn_g[0].reshape(1, D), ln_ffn_b[0].reshape(1, D), ln_ple_g[0].reshape(1, D),
                      ln_ple_b[0].reshape(1, D), B, S, cap)
    return out.reshape(B, S, D)
```

```python
import functools

import numpy as np
import jax
import jax.numpy as jnp
from jax import lax
from jax.experimental import pallas as pl
from jax.experimental.pallas import tpu as pltpu

F32 = jnp.float32
BF16 = jnp.bfloat16

D_MODEL = 1024
HEAD_DIM = 64
GQA_HEADS = 8
GQA_KV_HEADS = 2
MLA_HEADS = 8
MLA_NOPE_DIM = 64
MLA_ROPE_DIM = 32
MLA_V_DIM = 64
MLA_Q_RANK = 256
MLA_KV_RANK = 256
N_EXPERTS = 16
CAPACITY_FACTOR = 2
EXPERT_FF = 2816
PLE_DIM = 256
GRID_W = 64
ROPE_BASE = 10000.0
LN_EPS = 1e-5
RMS_EPS = 1e-6
DEPTH = 1
DN_ALPHA = (2 * DEPTH) ** 0.25
LOG2E = 1.4426950408889634

LANES = 128
SZ_Q = GQA_HEADS * HEAD_DIM
SZ_KV = GQA_KV_HEADS * HEAD_DIM
IN_COLS = SZ_Q + 2 * SZ_KV + MLA_Q_RANK + MLA_KV_RANK + LANES
OFF_K = SZ_Q
OFF_V = OFF_K + SZ_KV
OFF_CQ = OFF_V + SZ_KV
OFF_CKV = OFF_CQ + MLA_Q_RANK
OFF_KR = OFF_CKV + MLA_KV_RANK
N_CHUNKS = 12
N_PAIRS = 8
QKV_COLS = N_CHUNKS * LANES
MIX_WIDTH = N_PAIRS * LANES

TM_IN = 1024
IN_SUB = 256
TQ = 256
PAIRS_PER_STEP = 1
TM_OUT = 1024
OUT_SUB = 1024
FF_TILE = 256
FFN_ROWS = 2048
FFN_RB = 512
TS_FIN = 512
TOPK_SEQS = 8
TOK_BLK = 256
WIN = 64
WIN_ALIGN = 16
TAB_COLS = 16
TAB_FLAG = 8

VMEM_LIMIT = 56 * 1024 * 1024


def _cparams(n_axes, vmem=VMEM_LIMIT):
    return pltpu.CompilerParams(dimension_semantics=("arbitrary",) * n_axes, vmem_limit_bytes=vmem)


def _layer_norm(y, g, b):
    mu = jnp.mean(y, axis=-1, keepdims=True)
    yc = y - mu
    var = jnp.mean(yc * yc, axis=-1, keepdims=True)
    return yc * lax.rsqrt(var + LN_EPS) * g + b


def _split_bf16(v):
    hi = v.astype(BF16)
    lo = (v - hi.astype(F32)).astype(BF16)
    return hi, lo


def _rope(v, cos, sin, dist):
    lane = lax.broadcasted_iota(jnp.int32, v.shape, 1)
    first = (lane % (2 * dist)) < dist
    rot = jnp.where(first, pltpu.roll(v, LANES - dist, 1), pltpu.roll(v, dist, 1))
    return v * cos + rot * sin


def _head_sumsq(v, g_ref):
    hi, lo = _split_bf16(v * v)
    g = g_ref[...]
    return jnp.dot(hi, g, preferred_element_type=F32) + jnp.dot(lo, g, preferred_element_type=F32)


def _inproj_kernel(x_ref, win_ref, wuq_ref, wukv_ref, g512_ref, g128_ref, qg_ref, kg_ref, cqg_ref, ckvg_ref,
                   cos64_ref, sin64_ref, cosm_ref, sinm_ref, q_out, k_out, v_out):
    for r in range(x_ref.shape[0] // IN_SUB):
        _inproj_rows(slice(r * IN_SUB, (r + 1) * IN_SUB), x_ref, win_ref, wuq_ref, wukv_ref, g512_ref, g128_ref,
                     qg_ref, kg_ref, cqg_ref, ckvg_ref, cos64_ref, sin64_ref, cosm_ref, sinm_ref, q_out, k_out, v_out)


def _inproj_rows(rs, x_ref, win_ref, wuq_ref, wukv_ref, g512_ref, g128_ref, qg_ref, kg_ref, cqg_ref, ckvg_ref,
                 cos64_ref, sin64_ref, cosm_ref, sinm_ref, q_out, k_out, v_out):
    proj = jnp.dot(x_ref[rs, :].astype(BF16), win_ref[...], preferred_element_type=F32)
    cos64, sin64 = cos64_ref[rs, :], sin64_ref[rs, :]
    cosm, sinm = cosm_ref[rs, :], sinm_ref[rs, :]
    lane = lax.broadcasted_iota(jnp.int32, (proj.shape[0], LANES), 1)
    low = lane < HEAD_DIM

    qg = proj[:, :SZ_Q]
    qn = qg * lax.rsqrt(_head_sumsq(qg, g512_ref) * (1.0 / HEAD_DIM) + RMS_EPS) * qg_ref[...]
    for c in range(SZ_Q // LANES):
        sl = slice(c * LANES, (c + 1) * LANES)
        q_out[rs, sl] = _rope(qn[:, sl], cos64, sin64, HEAD_DIM // 4).astype(BF16)

    kg = proj[:, OFF_K:OFF_K + SZ_KV]
    kn = kg * lax.rsqrt(_head_sumsq(kg, g128_ref) * (1.0 / HEAD_DIM) + RMS_EPS) * kg_ref[...]
    kr = _rope(kn, cos64, sin64, HEAD_DIM // 4)
    vg = proj[:, OFF_V:OFF_V + SZ_KV]
    one_lo = jnp.where(lane == HEAD_DIM, 1.0, 0.0)
    one_hi = jnp.where(lane == 0, 1.0, 0.0)
    for src, dst, ones in ((kr, k_out, (0.0, 0.0)), (vg, v_out, (one_lo, one_hi))):
        lo = jnp.where(low, src, 0.0)
        hi = jnp.where(low, 0.0, src)
        dst[rs, 0 * LANES:1 * LANES] = (lo + ones[0]).astype(BF16)
        dst[rs, 1 * LANES:2 * LANES] = (pltpu.roll(lo, HEAD_DIM, 1) + ones[1]).astype(BF16)
        dst[rs, 2 * LANES:3 * LANES] = (pltpu.roll(hi, HEAD_DIM, 1) + ones[0]).astype(BF16)
        dst[rs, 3 * LANES:4 * LANES] = (hi + ones[1]).astype(BF16)

    cq = proj[:, OFF_CQ:OFF_CQ + MLA_Q_RANK]
    cqn = cq * lax.rsqrt(jnp.sum(cq * cq, axis=-1, keepdims=True) * (1.0 / MLA_Q_RANK) + RMS_EPS) * cqg_ref[...]
    qm = jnp.dot(cqn.astype(BF16), wuq_ref[...], preferred_element_type=F32)
    mla_scale = (MLA_NOPE_DIM + MLA_ROPE_DIM) ** -0.5 * LOG2E
    for h in range(MLA_HEADS):
        sl = slice(h * LANES, (h + 1) * LANES)
        dsl = slice(SZ_Q + h * LANES, SZ_Q + (h + 1) * LANES)
        q_out[rs, dsl] = (_rope(qm[:, sl], cosm, sinm, MLA_ROPE_DIM // 4) * mla_scale).astype(BF16)

    ckv = proj[:, OFF_CKV:OFF_CKV + MLA_KV_RANK]
    ckvn = ckv * lax.rsqrt(jnp.sum(ckv * ckv, axis=-1, keepdims=True) * (1.0 / MLA_KV_RANK) + RMS_EPS) * ckvg_ref[...]
    kv = jnp.dot(ckvn.astype(BF16), wukv_ref[...], preferred_element_type=F32)
    kpe = _rope(proj[:, OFF_KR:OFF_KR + LANES], cosm, sinm, MLA_ROPE_DIM // 4)
    for h in range(MLA_HEADS):
        sl = slice(h * LANES, (h + 1) * LANES)
        dsl = slice(SZ_Q + h * LANES, SZ_Q + (h + 1) * LANES)
        k_out[rs, dsl] = (kv[:, sl] + kpe).astype(BF16)
    voff = MLA_HEADS * LANES
    for j in range(MLA_HEADS // 2):
        vp = kv[:, voff + j * LANES: voff + (j + 1) * LANES]
        base = SZ_Q + 2 * j * LANES
        v_out[rs, base:base + LANES] = jnp.where(low, vp, one_lo).astype(BF16)
        v_out[rs, base + LANES:base + 2 * LANES] = jnp.where(low, one_hi, vp).astype(BF16)


def _inproj_call(x2d, win, wuq, wukv, g512, g128, qg, kg, cqg, ckvg, cos64, sin64, cosm, sinm, seq):
    T = x2d.shape[0]
    tm = TM_IN
    tiles_per_seq = seq // tm
    full = lambda shape: pl.BlockSpec(shape, lambda i: (0,) * len(shape))
    tab = pl.BlockSpec((tm, LANES), lambda i: (i % tiles_per_seq, 0))
    out_spec = pl.BlockSpec((tm, QKV_COLS), lambda i: (i, 0))
    out_shape = jax.ShapeDtypeStruct((T, QKV_COLS), BF16)
    return pl.pallas_call(
        _inproj_kernel,
        grid=(T // tm,),
        in_specs=[pl.BlockSpec((tm, D_MODEL), lambda i: (i, 0)),
                  full(win.shape), full(wuq.shape), full(wukv.shape), full(g512.shape), full(g128.shape),
                  full(qg.shape), full(kg.shape), full(cqg.shape), full(ckvg.shape),
                  tab, tab, tab, tab],
        out_specs=[out_spec, out_spec, out_spec],
        out_shape=[out_shape, out_shape, out_shape],
        compiler_params=_cparams(1),
        name="inproj",
    )(x2d, win, wuq, wukv, g512, g128, qg, kg, cqg, ckvg, cos64, sin64, cosm, sinm)


def _attn_kernel(*refs):
    o_ref = refs[-1]
    seq = o_ref.shape[0]

    def head(q, k_ref, v_ref):
        s = lax.dot_general(q, k_ref[...], (((1,), (1,)), ((), ())), preferred_element_type=F32)
        p = jnp.exp2(s - jnp.max(s, axis=-1, keepdims=True))
        return jnp.dot(p.astype(BF16), v_ref[...], preferred_element_type=F32)

    for j in range(PAIRS_PER_STEP):
        qe_ref, qo_ref, ke_ref, ko_ref, v_ref = refs[5 * j:5 * j + 5]

        def body(i, carry):
            r0 = pl.multiple_of(i * TQ, TQ)
            oe = head(qe_ref[pl.ds(r0, TQ), :], ke_ref, v_ref)[:, :LANES]
            oo = head(qo_ref[pl.ds(r0, TQ), :], ko_ref, v_ref)[:, LANES:]
            low = lax.broadcasted_iota(jnp.int32, (TQ, LANES), 1) < HEAD_DIM
            o = jnp.where(low, oe / oe[:, HEAD_DIM:HEAD_DIM + 1], oo / oo[:, 0:1])
            o_ref[pl.ds(r0, TQ), j * LANES:(j + 1) * LANES] = o.astype(BF16)
            return carry

        lax.fori_loop(0, seq // TQ, body, 0, unroll=True)


def _q_chunk(pr, odd):
    return jnp.where(pr < 4, pr, 4 + 2 * (pr - 4) + odd)


def _kv_chunk(pr, odd):
    return jnp.where(pr < 4, 2 * (pr // 2) + odd, 4 + 2 * (pr - 4) + odd)


def _attn_call(q_all, k_all, v_all, batch, seq):
    T = q_all.shape[0]
    blk = (seq, LANES)
    pps = PAIRS_PER_STEP
    in_specs, operands = [], []
    for j in range(pps):
        pair = lambda g, j=j: g * pps + j
        for odd in (0, 1):
            in_specs.append(pl.BlockSpec(blk, lambda b, g, odd=odd, pair=pair: (b, _q_chunk(pair(g), odd))))
            operands.append(q_all)
        for odd in (0, 1):
            in_specs.append(pl.BlockSpec(blk, lambda b, g, odd=odd, pair=pair: (b, _kv_chunk(pair(g), odd))))
            operands.append(k_all)
        in_specs.append(pl.BlockSpec((seq, 2 * LANES), lambda b, g, pair=pair: (b, _kv_chunk(pair(g), 0) // 2)))
        operands.append(v_all)
    return pl.pallas_call(
        _attn_kernel,
        grid=(batch, N_PAIRS // pps),
        in_specs=in_specs,
        out_specs=pl.BlockSpec((seq, pps * LANES), lambda b, g: (b, g)),
        out_shape=jax.ShapeDtypeStruct((T, MIX_WIDTH), BF16),
        compiler_params=_cparams(2),
        name="attention",
    )(*operands)


def _outproj_kernel(o_ref, x_ref, wout_ref, g_ref, b_ref, wr_ref, x1_ref, x1b_ref, aff_ref):
    for r in range(x_ref.shape[0] // OUT_SUB):
        rs = slice(r * OUT_SUB, (r + 1) * OUT_SUB)
        mix = jnp.dot(o_ref[rs, :], wout_ref[...], preferred_element_type=F32)
        x1 = _layer_norm(DN_ALPHA * x_ref[rs, :] + mix, g_ref[...], b_ref[...])
        x1_ref[rs, :] = x1
        hi, lo = _split_bf16(x1)
        x1b_ref[rs, :] = hi
        wr = wr_ref[...]
        la = jnp.dot(hi, wr, preferred_element_type=F32)
        lb = jnp.dot(lo, wr, preferred_element_type=F32)
        logits = (la + pltpu.roll(la, LANES - N_EXPERTS, 1) + lb).T[:N_EXPERTS, :]
        e = jnp.exp(logits - jnp.max(logits, axis=0, keepdims=True))
        aff_ref[:, rs] = e / jnp.sum(e, axis=0, keepdims=True)


def _outproj_call(o_mix, x2d, wout, g, b, wr):
    T = x2d.shape[0]
    tm = TM_OUT
    full = lambda shape: pl.BlockSpec(shape, lambda i: (0,) * len(shape))
    row = lambda w: pl.BlockSpec((tm, w), lambda i: (i, 0))
    return pl.pallas_call(
        _outproj_kernel,
        grid=(T // tm,),
        in_specs=[row(MIX_WIDTH), row(D_MODEL), full(wout.shape), full(g.shape), full(b.shape), full(wr.shape)],
        out_specs=[row(D_MODEL), row(D_MODEL), pl.BlockSpec((N_EXPERTS, tm), lambda i: (0, i))],
        out_shape=[jax.ShapeDtypeStruct((T, D_MODEL), F32), jax.ShapeDtypeStruct((T, D_MODEL), BF16),
                   jax.ShapeDtypeStruct((N_EXPERTS, T), F32)],
        compiler_params=_cparams(1),
        name="outproj_router",
    )(o_mix, x2d, wout, g, b, wr)


def _excl_cumsum(mask, upper):
    n = mask.shape[1]
    ones = jnp.where(mask, 1.0, 0.0)
    run = jnp.zeros((mask.shape[0], 1), F32)
    outs = []
    for j in range(n // LANES):
        blk = ones[:, j * LANES:(j + 1) * LANES]
        outs.append(jnp.dot(blk.astype(BF16), upper, preferred_element_type=F32) + run)
        run = run + jnp.sum(blk, axis=1, keepdims=True)
    return jnp.concatenate(outs, axis=1)


def _topk_kernel(cap, seq, aff_ref, pos_ref, post_ref, tab_ref):
    nb, n_e = pos_ref.shape[0], pos_ref.shape[1]
    bits_of = lambda j: lax.bitcast_convert_type(aff_ref[:, j * seq:(j + 1) * seq], jnp.int32)

    def step(k, ts):
        bit = jnp.left_shift(jnp.int32(1), 30 - k)
        out = []
        for j in range(nb):
            cand = ts[j] | bit
            cnt = jnp.sum(jnp.where(bits_of(j) >= cand, 1.0, 0.0), axis=1, keepdims=True)
            out.append(jnp.where(cnt >= cap, cand, ts[j]))
        return tuple(out)

    ts = lax.fori_loop(0, 31, step, tuple(jnp.zeros((n_e, 1), jnp.int32) for _ in range(nb)))
    ri = lax.broadcasted_iota(jnp.int32, (LANES, LANES), 0)
    ci = lax.broadcasted_iota(jnp.int32, (LANES, LANES), 1)
    upper = jnp.where(ri < ci, 1.0, 0.0).astype(BF16)
    lane = lax.broadcasted_iota(jnp.int32, (n_e, LANES), 1)
    for j in range(nb):
        bits, t = bits_of(j), ts[j]
        gt = bits > t
        eq = bits == t
        need = cap - jnp.sum(jnp.where(gt, 1.0, 0.0), axis=1, keepdims=True)
        sel = gt | (eq & (_excl_cumsum(eq, upper) < need))
        pos = jnp.where(sel, _excl_cumsum(sel, upper), -1.0)
        pos_ref[j] = pos
        padded = jnp.concatenate([pos, jnp.full((LANES - n_e, seq), -1.0, F32)], axis=0)
        post_ref[j] = padded.T

        ones = jnp.where(sel, 1.0, 0.0)
        lo = jnp.zeros((n_e, 1), F32)
        tab = jnp.zeros((n_e, LANES), F32)
        over = jnp.zeros((n_e, 1), F32)
        for kb in range(seq // TOK_BLK):
            hi = lo + jnp.sum(ones[:, kb * TOK_BLK:(kb + 1) * TOK_BLK], axis=1, keepdims=True)
            win = jnp.minimum(jnp.floor(lo * (1.0 / WIN_ALIGN)) * WIN_ALIGN, float(cap - WIN))
            over = jnp.maximum(over, jnp.where(hi - win > WIN, 1.0, 0.0))
            tab = jnp.where(lane == kb, win, tab)
            lo = hi
        tab = jnp.where(lane == TAB_FLAG, jnp.max(over, axis=0, keepdims=True), tab)
        tab_ref[j] = tab.astype(jnp.int32)


def _topk_call(aff_t, batch, seq, cap):
    assert seq // TOK_BLK <= TAB_FLAG < TAB_COLS and cap % WIN == 0 and WIN % WIN_ALIGN == 0
    nb = TOPK_SEQS
    return pl.pallas_call(
        functools.partial(_topk_kernel, cap, seq),
        grid=(batch // nb,),
        in_specs=[pl.BlockSpec((N_EXPERTS, nb * seq), lambda b: (0, b))],
        out_specs=[pl.BlockSpec((nb, N_EXPERTS, seq), lambda b: (b, 0, 0)),
                   pl.BlockSpec((nb, seq, LANES), lambda b: (b, 0, 0)),
                   pl.BlockSpec((nb, N_EXPERTS, LANES), lambda b: (b, 0, 0))],
        out_shape=[jax.ShapeDtypeStruct((batch, N_EXPERTS, seq), F32),
                   jax.ShapeDtypeStruct((batch, seq, LANES), F32),
                   jax.ShapeDtypeStruct((batch, N_EXPERTS, LANES), jnp.int32)],
        compiler_params=_cparams(1),
        name="expert_choice",
    )(aff_t)


def _tab_at(tab_ref, b, e, col):
    return tab_ref[(b * N_EXPERTS + e) * TAB_COLS + col]


def _gather_kernel(cap, tab_ref, pos_ref, aff_ref, x_ref, xg_ref, gs_ref):
    b = pl.program_id(0)
    seq = x_ref.shape[0]
    fits = _tab_at(tab_ref, b, 0, TAB_FLAG) == 0

    @pl.when(fits)
    def _():
        xg_ref[...] = jnp.zeros_like(xg_ref)
        gs_ref[...] = jnp.zeros_like(gs_ref)
        wslot = lax.broadcasted_iota(jnp.int32, (WIN, 1), 0)
        for kb in range(seq // TOK_BLK):
            ts = slice(kb * TOK_BLK, (kb + 1) * TOK_BLK)
            wins, hots = [], []
            for e in range(N_EXPERTS):
                win = pl.multiple_of(_tab_at(tab_ref, b, e, kb), WIN_ALIGN)
                hit = pos_ref[0, e:e + 1, ts] == (wslot + win).astype(F32)
                g = jnp.sum(jnp.where(hit, aff_ref[e:e + 1, ts], 0.0), axis=1, keepdims=True)
                gs_ref[e, pl.ds(win, WIN), :] += jnp.broadcast_to(g, (WIN, LANES))
                wins.append(win)
                hots.append(jnp.where(hit, 1.0, 0.0).astype(BF16))
            rows = jnp.dot(jnp.concatenate(hots, axis=0), x_ref[ts, :], preferred_element_type=F32)
            for e in range(N_EXPERTS):
                xg_ref[e, pl.ds(wins[e], WIN), :] += rows[e * WIN:(e + 1) * WIN].astype(BF16)

    @pl.when(jnp.logical_not(fits))
    def _():
        slot = lax.broadcasted_iota(jnp.int32, (cap, seq), 0).astype(F32)

        def body(e, carry):
            onehot = pos_ref[0, pl.ds(e, 1), :] == slot
            xg = jnp.dot(jnp.where(onehot, 1.0, 0.0).astype(BF16), x_ref[...], preferred_element_type=F32)
            xg_ref[e] = xg.astype(BF16)
            g = jnp.sum(jnp.where(onehot, aff_ref[pl.ds(e, 1), :], 0.0), axis=1, keepdims=True)
            gs_ref[e] = jnp.broadcast_to(g, (cap, LANES))
            return carry

        lax.fori_loop(0, N_EXPERTS, body, 0)


def _gather_call(tab, pos, aff_t, x1b, batch, seq, cap):
    grid_spec = pltpu.PrefetchScalarGridSpec(
        num_scalar_prefetch=1,
        grid=(batch,),
        in_specs=[pl.BlockSpec((1, N_EXPERTS, seq), lambda b, tab: (b, 0, 0)),
                  pl.BlockSpec((N_EXPERTS, seq), lambda b, tab: (0, b)),
                  pl.BlockSpec((seq, D_MODEL), lambda b, tab: (b, 0))],
        out_specs=[pl.BlockSpec((N_EXPERTS, cap, D_MODEL), lambda b, tab: (0, b, 0)),
                   pl.BlockSpec((N_EXPERTS, cap, LANES), lambda b, tab: (0, b, 0))],
    )
    return pl.pallas_call(
        functools.partial(_gather_kernel, cap),
        grid_spec=grid_spec,
        out_shape=[jax.ShapeDtypeStruct((N_EXPERTS, batch * cap, D_MODEL), BF16),
                   jax.ShapeDtypeStruct((N_EXPERTS, batch * cap, LANES), F32)],
        compiler_params=_cparams(1),
        name="moe_gather",
    )(tab, pos, aff_t, x1b)


def _ffn_kernel(xg_ref, gs_ref, wg_ref, wu_ref, wd_ref, y_ref, acc_ref, wgb_ref, wub_ref, wdb_ref):
    f = pl.program_id(2)
    wgb_ref[...] = wg_ref[0].astype(BF16)
    wub_ref[...] = wu_ref[0].astype(BF16)
    wdb_ref[...] = wd_ref[0].astype(BF16)

    @pl.when(f == 0)
    def _():
        acc_ref[...] = jnp.zeros_like(acc_ref)

    def body(i, carry):
        r0 = pl.multiple_of(i * FFN_RB, FFN_RB)
        xb = xg_ref[0, pl.ds(r0, FFN_RB), :]
        g = jnp.dot(xb, wgb_ref[...], preferred_element_type=F32)
        u = jnp.dot(xb, wub_ref[...], preferred_element_type=F32)
        h = (g * jax.nn.sigmoid(g) * u).astype(BF16)
        acc_ref[pl.ds(r0, FFN_RB), :] += jnp.dot(h, wdb_ref[...], preferred_element_type=F32)
        return carry

    lax.fori_loop(0, FFN_ROWS // FFN_RB, body, 0, unroll=True)

    @pl.when(f == pl.num_programs(2) - 1)
    def _():
        y_ref[0] = (acc_ref[...] * gs_ref[0][:, :1]).astype(BF16)


def _ffn_call(xg, gs, w_gate, w_up, w_down):
    n_e, rows, _ = xg.shape
    return pl.pallas_call(
        _ffn_kernel,
        grid=(n_e, rows // FFN_ROWS, EXPERT_FF // FF_TILE),
        in_specs=[pl.BlockSpec((1, FFN_ROWS, D_MODEL), lambda e, m, f: (e, m, 0)),
                  pl.BlockSpec((1, FFN_ROWS, LANES), lambda e, m, f: (e, m, 0)),
                  pl.BlockSpec((1, D_MODEL, FF_TILE), lambda e, m, f: (e, 0, f)),
                  pl.BlockSpec((1, D_MODEL, FF_TILE), lambda e, m, f: (e, 0, f)),
                  pl.BlockSpec((1, FF_TILE, D_MODEL), lambda e, m, f: (e, f, 0))],
        out_specs=pl.BlockSpec((1, FFN_ROWS, D_MODEL), lambda e, m, f: (e, m, 0)),
        out_shape=jax.ShapeDtypeStruct((n_e, rows, D_MODEL), BF16),
        scratch_shapes=[pltpu.VMEM((FFN_ROWS, D_MODEL), F32),
                        pltpu.VMEM((D_MODEL, FF_TILE), BF16),
                        pltpu.VMEM((D_MODEL, FF_TILE), BF16),
                        pltpu.VMEM((FF_TILE, D_MODEL), BF16)],
        compiler_params=_cparams(3),
        name="moe_ffn",
    )(xg, gs, w_gate, w_up, w_down)


def _final_kernel(cap, tab_ref, post_ref, yg_ref, x1_ref, p_ref, wpp_ref, wpg_ref, g2_ref, b2_ref, g3_ref, b3_ref,
                  out_ref, moe_ref):
    b = pl.program_id(0)
    ts = moe_ref.shape[0]
    fits = _tab_at(tab_ref, b, 0, TAB_FLAG) == 0
    per_tile = TOK_BLK // WIN

    @pl.when(fits)
    def _():
        lane = lax.broadcasted_iota(jnp.int32, (1, TOK_BLK), 1)
        grp = lane // WIN
        for j in range(ts // TOK_BLK):
            kb = pl.program_id(1) * (ts // TOK_BLK) + j
            post = post_ref[0, j * TOK_BLK:(j + 1) * TOK_BLK, :]
            acc = jnp.zeros((TOK_BLK, D_MODEL), F32)
            for e0 in range(0, N_EXPERTS, per_tile):
                wins = [pl.multiple_of(_tab_at(tab_ref, b, e0 + r, kb), WIN_ALIGN) for r in range(per_tile)]
                rhs = jnp.concatenate([yg_ref[e0 + r, pl.ds(wins[r], WIN), :] for r in range(per_tile)], axis=0)
                tgt = (lane - grp * WIN).astype(F32)
                col = jnp.zeros((TOK_BLK, TOK_BLK), F32)
                for r in range(per_tile):
                    tgt = tgt + jnp.where(grp == r, wins[r].astype(F32), 0.0)
                    col = jnp.where(grp == r, post[:, e0 + r:e0 + r + 1], col)
                onehot = jnp.where(col == tgt, 1.0, 0.0).astype(BF16)
                acc = acc + jnp.dot(onehot, rhs, preferred_element_type=F32)
            moe_ref[j * TOK_BLK:(j + 1) * TOK_BLK, :] = acc

    @pl.when(jnp.logical_not(fits))
    def _():
        slot = lax.broadcasted_iota(jnp.int32, (ts, cap), 1).astype(F32)

        def body(e, moe):
            lane_e = lax.broadcasted_iota(jnp.int32, (1, LANES), 1) == e
            col = jnp.sum(jnp.where(lane_e, post_ref[0], 0.0), axis=1, keepdims=True)
            onehot = jnp.where(col == slot, 1.0, 0.0).astype(BF16)
            return moe + jnp.dot(onehot, yg_ref[e], preferred_element_type=F32)

        moe_ref[...] = lax.fori_loop(0, N_EXPERTS, body, jnp.zeros((ts, D_MODEL), F32))

    x2 = _layer_norm(DN_ALPHA * x1_ref[...] + moe_ref[...], g2_ref[...], b2_ref[...])
    emb = jnp.dot(p_ref[...].astype(BF16), wpp_ref[...], preferred_element_type=F32)
    gate = jax.nn.sigmoid(jnp.dot(x2.astype(BF16), wpg_ref[...], preferred_element_type=F32))
    out_ref[...] = _layer_norm(DN_ALPHA * x2 + gate * emb, g3_ref[...], b3_ref[...])


def _final_call(tab, post, yg, x1, p2d, wpp, wpg, g2, b2, g3, b3, batch, seq, cap):
    T = x1.shape[0]
    ts = TS_FIN
    nt = seq // ts
    full = lambda shape: pl.BlockSpec(shape, lambda b, i, tab: (0,) * len(shape))
    row = lambda w: pl.BlockSpec((ts, w), lambda b, i, tab: (b * nt + i, 0))
    grid_spec = pltpu.PrefetchScalarGridSpec(
        num_scalar_prefetch=1,
        grid=(batch, nt),
        in_specs=[pl.BlockSpec((1, ts, LANES), lambda b, i, tab: (b, i, 0)),
                  pl.BlockSpec((N_EXPERTS, cap, D_MODEL), lambda b, i, tab: (0, b, 0)),
                  row(D_MODEL), row(PLE_DIM), full(wpp.shape), full(wpg.shape),
                  full(g2.shape), full(b2.shape), full(g3.shape), full(b3.shape)],
        out_specs=row(D_MODEL),
        scratch_shapes=[pltpu.VMEM((ts, D_MODEL), F32)],
    )
    return pl.pallas_call(
        functools.partial(_final_kernel, cap),
        grid_spec=grid_spec,
        out_shape=jax.ShapeDtypeStruct((T, D_MODEL), F32),
        compiler_params=_cparams(2),
        name="combine_ple",
    )(tab, post, yg, x1, p2d, wpp, wpg, g2, b2, g3, b3)


def _rope_tables(seq):
    rows = seq // GRID_W
    row = jnp.repeat(jnp.arange(rows, dtype=F32), GRID_W)
    col = jnp.tile(jnp.arange(GRID_W, dtype=F32), rows)

    def table(head_dim, lane_of):
        sub = head_dim // 2
        half = sub // 2
        inv_freq = ROPE_BASE ** (-jnp.arange(half, dtype=F32) * 2.0 / sub)
        ang_row = row[:, None] * inv_freq[None, :]
        ang_col = col[:, None] * inv_freq[None, :]
        ang = jnp.concatenate([ang_row, ang_col], axis=1)
        idx = np.where(lane_of < 0, 0, (lane_of // sub) * half + (lane_of % sub) % half)
        first = (lane_of % sub) < half
        active = lane_of >= 0
        a = ang[:, idx]
        cos = jnp.where(active[None, :], jnp.cos(a), 1.0)
        sin = jnp.where(active[None, :], jnp.where(first[None, :], -jnp.sin(a), jnp.sin(a)), 0.0)
        return cos, sin

    lanes = np.arange(LANES)
    cos64, sin64 = table(HEAD_DIM, lanes % HEAD_DIM)
    mla_lane = np.where((lanes >= MLA_NOPE_DIM) & (lanes < MLA_NOPE_DIM + MLA_ROPE_DIM), lanes - MLA_NOPE_DIM, -1)
    cosm, sinm = table(MLA_ROPE_DIM, mla_lane)
    return cos64, sin64, cosm, sinm


def _block_ones(n, blk):
    i = np.arange(n) // blk
    return jnp.asarray((i[:, None] == i[None, :]).astype(np.float32), dtype=BF16)


def kernel(x, p, w_in, q_norm, k_norm, cq_norm, ckv_norm, w_uq, w_ukv, w_out, ln_attn_g, ln_attn_b, w_router,
           w_gate, w_up, w_down, ln_ffn_g, ln_ffn_b, w_ple_proj, w_ple_gate, ln_ple_g, ln_ple_b):
    assert w_in.shape[0] == DEPTH
    B, S, D = x.shape
    T = B * S
    cap = CAPACITY_FACTOR * S // N_EXPERTS
    x2d = x.reshape(T, D)

    wi = w_in[0]
    zc = lambda n: jnp.zeros((D, n), F32)
    win = jnp.concatenate([wi[:, :OFF_KR], zc(MLA_NOPE_DIM), wi[:, OFF_KR:], zc(LANES - MLA_NOPE_DIM - MLA_ROPE_DIM)],
                          axis=1).astype(BF16)
    qk = MLA_NOPE_DIM + MLA_ROPE_DIM
    wuq = jnp.pad(w_uq[0].reshape(MLA_Q_RANK, MLA_HEADS, qk), ((0, 0), (0, 0), (0, LANES - qk)))
    wuq = wuq.reshape(MLA_Q_RANK, MLA_HEADS * LANES).astype(BF16)
    wkv = w_ukv[0].reshape(MLA_KV_RANK, MLA_HEADS, MLA_NOPE_DIM + MLA_V_DIM)
    wk = jnp.pad(wkv[:, :, :MLA_NOPE_DIM], ((0, 0), (0, 0), (0, LANES - MLA_NOPE_DIM)))
    wukv = jnp.concatenate([wk.reshape(MLA_KV_RANK, MLA_HEADS * LANES),
                            wkv[:, :, MLA_NOPE_DIM:].reshape(MLA_KV_RANK, MLA_HEADS * MLA_V_DIM)], axis=1).astype(BF16)
    qg = (jnp.tile(q_norm[0], GQA_HEADS) * (HEAD_DIM ** -0.5 * LOG2E)).reshape(1, SZ_Q)
    kg = jnp.tile(k_norm[0], GQA_KV_HEADS).reshape(1, SZ_KV)
    cqg = cq_norm[0].reshape(1, MLA_Q_RANK)
    ckvg = ckv_norm[0].reshape(1, MLA_KV_RANK)
    cos64, sin64, cosm, sinm = _rope_tables(S)

    q_all, k_all, v_all = _inproj_call(x2d, win, wuq, wukv, _block_ones(SZ_Q, HEAD_DIM), _block_ones(SZ_KV, HEAD_DIM),
                                       qg, kg, cqg, ckvg, cos64, sin64, cosm, sinm, S)
    o_mix = _attn_call(q_all, k_all, v_all, B, S)

    wrh = w_router[0].astype(BF16)
    wrl = (w_router[0] - wrh.astype(F32)).astype(BF16)
    wr = jnp.concatenate([wrh, wrl, jnp.zeros((D, LANES - 2 * N_EXPERTS), BF16)], axis=1)
    x1, x1b, aff_t = _outproj_call(o_mix, x2d, w_out[0].astype(BF16), ln_attn_g[0].reshape(1, D),
                                   ln_attn_b[0].reshape(1, D), wr)

    pos, post, tab = _topk_call(aff_t, B, S, cap)
    tab = tab[:, :, :TAB_COLS].reshape(-1)
    xg, gs = _gather_call(tab, pos, aff_t, x1b, B, S, cap)
    yg = _ffn_call(xg, gs, w_gate[0], w_up[0], w_down[0])
    out = _final_call(tab, post, yg, x1, p[0].reshape(T, PLE_DIM), w_ple_proj[0].astype(BF16), w_ple_gate[0].astype(BF16),
                      ln_ffn_g[0].reshape(1, D), ln_ffn_b[0].reshape(1, D), ln_ple_g[0].reshape(1, D),
                      ln_ple_b[0].reshape(1, D), B, S, cap)
    return out.reshape(B, S, D)
```

```python
import functools

import numpy as np
import jax
import jax.numpy as jnp
from jax import lax
from jax.experimental import pallas as pl
from jax.experimental.pallas import tpu as pltpu

F32 = jnp.float32
BF16 = jnp.bfloat16

D_MODEL = 1024
HEAD_DIM = 64
GQA_HEADS = 8
GQA_KV_HEADS = 2
MLA_HEADS = 8
MLA_NOPE_DIM = 64
MLA_ROPE_DIM = 32
MLA_V_DIM = 64
MLA_Q_RANK = 256
MLA_KV_RANK = 256
N_EXPERTS = 16
CAPACITY_FACTOR = 2
EXPERT_FF = 2816
PLE_DIM = 256
GRID_W = 64
ROPE_BASE = 10000.0
LN_EPS = 1e-5
RMS_EPS = 1e-6
DEPTH = 1
DN_ALPHA = (2 * DEPTH) ** 0.25
LOG2E = 1.4426950408889634

LANES = 128
SZ_Q = GQA_HEADS * HEAD_DIM
SZ_KV = GQA_KV_HEADS * HEAD_DIM
IN_COLS = SZ_Q + 2 * SZ_KV + MLA_Q_RANK + MLA_KV_RANK + LANES
OFF_K = SZ_Q
OFF_V = OFF_K + SZ_KV
OFF_CQ = OFF_V + SZ_KV
OFF_CKV = OFF_CQ + MLA_Q_RANK
OFF_KR = OFF_CKV + MLA_KV_RANK
N_CHUNKS = 12
N_PAIRS = 8
QKV_COLS = N_CHUNKS * LANES
MIX_WIDTH = N_PAIRS * LANES

TM_IN = 1024
IN_SUB = 256
TQ = 256
PAIRS_PER_STEP = 2
TM_OUT = 1024
OUT_SUB = 1024
FF_TILE = 256
FFN_ROWS = 2048
FFN_RB = 512
TS_FIN = 512
TOPK_SEQS = 8
TOK_BLK = 256
WIN = 64
WIN_ALIGN = 16
TAB_COLS = 16
TAB_FLAG = 8

VMEM_LIMIT = 56 * 1024 * 1024


def _cparams(n_axes, vmem=VMEM_LIMIT):
    return pltpu.CompilerParams(dimension_semantics=("arbitrary",) * n_axes, vmem_limit_bytes=vmem)


def _layer_norm(y, g, b):
    mu = jnp.mean(y, axis=-1, keepdims=True)
    yc = y - mu
    var = jnp.mean(yc * yc, axis=-1, keepdims=True)
    return yc * lax.rsqrt(var + LN_EPS) * g + b


def _split_bf16(v):
    hi = v.astype(BF16)
    lo = (v - hi.astype(F32)).astype(BF16)
    return hi, lo


def _rope(v, cos, sin, dist):
    lane = lax.broadcasted_iota(jnp.int32, v.shape, 1)
    first = (lane % (2 * dist)) < dist
    rot = jnp.where(first, pltpu.roll(v, LANES - dist, 1), pltpu.roll(v, dist, 1))
    return v * cos + rot * sin


def _head_sumsq(v, g_ref):
    hi, lo = _split_bf16(v * v)
    g = g_ref[...]
    return jnp.dot(hi, g, preferred_element_type=F32) + jnp.dot(lo, g, preferred_element_type=F32)


def _inproj_kernel(x_ref, win_ref, wuq_ref, wukv_ref, g512_ref, g128_ref, qg_ref, kg_ref, cqg_ref, ckvg_ref,
                   cos64_ref, sin64_ref, cosm_ref, sinm_ref, q_out, k_out, v_out):
    for r in range(x_ref.shape[0] // IN_SUB):
        _inproj_rows(slice(r * IN_SUB, (r + 1) * IN_SUB), x_ref, win_ref, wuq_ref, wukv_ref, g512_ref, g128_ref,
                     qg_ref, kg_ref, cqg_ref, ckvg_ref, cos64_ref, sin64_ref, cosm_ref, sinm_ref, q_out, k_out, v_out)


def _inproj_rows(rs, x_ref, win_ref, wuq_ref, wukv_ref, g512_ref, g128_ref, qg_ref, kg_ref, cqg_ref, ckvg_ref,
                 cos64_ref, sin64_ref, cosm_ref, sinm_ref, q_out, k_out, v_out):
    proj = jnp.dot(x_ref[rs, :].astype(BF16), win_ref[...], preferred_element_type=F32)
    cos64, sin64 = cos64_ref[rs, :], sin64_ref[rs, :]
    cosm, sinm = cosm_ref[rs, :], sinm_ref[rs, :]
    lane = lax.broadcasted_iota(jnp.int32, (proj.shape[0], LANES), 1)
    low = lane < HEAD_DIM

    qg = proj[:, :SZ_Q]
    qn = qg * lax.rsqrt(_head_sumsq(qg, g512_ref) * (1.0 / HEAD_DIM) + RMS_EPS) * qg_ref[...]
    for c in range(SZ_Q // LANES):
        sl = slice(c * LANES, (c + 1) * LANES)
        q_out[rs, sl] = _rope(qn[:, sl], cos64, sin64, HEAD_DIM // 4).astype(BF16)

    kg = proj[:, OFF_K:OFF_K + SZ_KV]
    kn = kg * lax.rsqrt(_head_sumsq(kg, g128_ref) * (1.0 / HEAD_DIM) + RMS_EPS) * kg_ref[...]
    kr = _rope(kn, cos64, sin64, HEAD_DIM // 4)
    vg = proj[:, OFF_V:OFF_V + SZ_KV]
    one_lo = jnp.where(lane == HEAD_DIM, 1.0, 0.0)
    one_hi = jnp.where(lane == 0, 1.0, 0.0)
    for src, dst, fill_lo, fill_hi in ((kr, k_out, 0.0, 0.0), (vg, v_out, one_lo, one_hi)):
        lo = jnp.where(low, src, fill_lo)
        hi = jnp.where(low, fill_hi, src)
        dst[rs, 0 * LANES:1 * LANES] = lo.astype(BF16)
        dst[rs, 1 * LANES:2 * LANES] = pltpu.roll(lo, HEAD_DIM, 1).astype(BF16)
        dst[rs, 2 * LANES:3 * LANES] = pltpu.roll(hi, HEAD_DIM, 1).astype(BF16)
        dst[rs, 3 * LANES:4 * LANES] = hi.astype(BF16)

    cq = proj[:, OFF_CQ:OFF_CQ + MLA_Q_RANK]
    cqn = cq * lax.rsqrt(jnp.sum(cq * cq, axis=-1, keepdims=True) * (1.0 / MLA_Q_RANK) + RMS_EPS) * cqg_ref[...]
    qm = jnp.dot(cqn.astype(BF16), wuq_ref[...], preferred_element_type=F32)
    mla_scale = (MLA_NOPE_DIM + MLA_ROPE_DIM) ** -0.5 * LOG2E
    for h in range(MLA_HEADS):
        sl = slice(h * LANES, (h + 1) * LANES)
        dsl = slice(SZ_Q + h * LANES, SZ_Q + (h + 1) * LANES)
        q_out[rs, dsl] = (_rope(qm[:, sl], cosm, sinm, MLA_ROPE_DIM // 4) * mla_scale).astype(BF16)

    ckv = proj[:, OFF_CKV:OFF_CKV + MLA_KV_RANK]
    ckvn = ckv * lax.rsqrt(jnp.sum(ckv * ckv, axis=-1, keepdims=True) * (1.0 / MLA_KV_RANK) + RMS_EPS) * ckvg_ref[...]
    kv = jnp.dot(ckvn.astype(BF16), wukv_ref[...], preferred_element_type=F32)
    kpe = _rope(proj[:, OFF_KR:OFF_KR + LANES], cosm, sinm, MLA_ROPE_DIM // 4)
    for h in range(MLA_HEADS):
        sl = slice(h * LANES, (h + 1) * LANES)
        dsl = slice(SZ_Q + h * LANES, SZ_Q + (h + 1) * LANES)
        k_out[rs, dsl] = (kv[:, sl] + kpe).astype(BF16)
    voff = MLA_HEADS * LANES
    for j in range(MLA_HEADS // 2):
        vp = kv[:, voff + j * LANES: voff + (j + 1) * LANES]
        base = SZ_Q + 2 * j * LANES
        v_out[rs, base:base + LANES] = jnp.where(low, vp, one_lo).astype(BF16)
        v_out[rs, base + LANES:base + 2 * LANES] = jnp.where(low, one_hi, vp).astype(BF16)


def _inproj_call(x2d, win, wuq, wukv, g512, g128, qg, kg, cqg, ckvg, cos64, sin64, cosm, sinm, seq):
    T = x2d.shape[0]
    tm = TM_IN
    tiles_per_seq = seq // tm
    full = lambda shape: pl.BlockSpec(shape, lambda i: (0,) * len(shape))
    tab = pl.BlockSpec((tm, LANES), lambda i: (i % tiles_per_seq, 0))
    out_spec = pl.BlockSpec((tm, QKV_COLS), lambda i: (i, 0))
    out_shape = jax.ShapeDtypeStruct((T, QKV_COLS), BF16)
    return pl.pallas_call(
        _inproj_kernel,
        grid=(T // tm,),
        in_specs=[pl.BlockSpec((tm, D_MODEL), lambda i: (i, 0)),
                  full(win.shape), full(wuq.shape), full(wukv.shape), full(g512.shape), full(g128.shape),
                  full(qg.shape), full(kg.shape), full(cqg.shape), full(ckvg.shape),
                  tab, tab, tab, tab],
        out_specs=[out_spec, out_spec, out_spec],
        out_shape=[out_shape, out_shape, out_shape],
        compiler_params=_cparams(1),
        name="inproj",
    )(x2d, win, wuq, wukv, g512, g128, qg, kg, cqg, ckvg, cos64, sin64, cosm, sinm)


def _attn_kernel(*refs):
    o_ref = refs[-1]
    seq = o_ref.shape[0]

    def head(q, k_ref, v_ref):
        hk = seq // 2
        nt = (((1,), (1,)), ((), ()))
        s0 = lax.dot_general(q, k_ref[:hk, :], nt, preferred_element_type=F32)
        s1 = lax.dot_general(q, k_ref[hk:, :], nt, preferred_element_type=F32)
        m = jnp.maximum(jnp.max(s0, axis=-1, keepdims=True), jnp.max(s1, axis=-1, keepdims=True))
        p0 = jnp.exp2(s0 - m).astype(BF16)
        p1 = jnp.exp2(s1 - m).astype(BF16)
        return (jnp.dot(p0, v_ref[:hk, :], preferred_element_type=F32)
                + jnp.dot(p1, v_ref[hk:, :], preferred_element_type=F32))

    for j in range(PAIRS_PER_STEP):
        qe_ref, qo_ref, ke_ref, ko_ref, v_ref = refs[5 * j:5 * j + 5]

        def body(i, carry):
            r0 = pl.multiple_of(i * TQ, TQ)
            oe = head(qe_ref[pl.ds(r0, TQ), :], ke_ref, v_ref)[:, :LANES]
            oo = head(qo_ref[pl.ds(r0, TQ), :], ko_ref, v_ref)[:, LANES:]
            low = lax.broadcasted_iota(jnp.int32, (TQ, LANES), 1) < HEAD_DIM
            o = jnp.where(low, oe / oe[:, HEAD_DIM:HEAD_DIM + 1], oo / oo[:, 0:1])
            o_ref[pl.ds(r0, TQ), j * LANES:(j + 1) * LANES] = o.astype(BF16)
            return carry

        lax.fori_loop(0, seq // TQ, body, 0, unroll=True)


def _q_chunk(pr, odd):
    return jnp.where(pr < 4, pr, 4 + 2 * (pr - 4) + odd)


def _kv_chunk(pr, odd):
    return jnp.where(pr < 4, 2 * (pr // 2) + odd, 4 + 2 * (pr - 4) + odd)


def _attn_call(q_all, k_all, v_all, batch, seq):
    T = q_all.shape[0]
    blk = (seq, LANES)
    pps = PAIRS_PER_STEP
    in_specs, operands = [], []
    for j in range(pps):
        pair = lambda g, j=j: g * pps + j
        for odd in (0, 1):
            in_specs.append(pl.BlockSpec(blk, lambda b, g, odd=odd, pair=pair: (b, _q_chunk(pair(g), odd))))
            operands.append(q_all)
        for odd in (0, 1):
            in_specs.append(pl.BlockSpec(blk, lambda b, g, odd=odd, pair=pair: (b, _kv_chunk(pair(g), odd))))
            operands.append(k_all)
        in_specs.append(pl.BlockSpec((seq, 2 * LANES), lambda b, g, pair=pair: (b, _kv_chunk(pair(g), 0) // 2)))
        operands.append(v_all)
    return pl.pallas_call(
        _attn_kernel,
        grid=(batch, N_PAIRS // pps),
        in_specs=in_specs,
        out_specs=pl.BlockSpec((seq, pps * LANES), lambda b, g: (b, g)),
        out_shape=jax.ShapeDtypeStruct((T, MIX_WIDTH), BF16),
        compiler_params=_cparams(2),
        name="attention",
    )(*operands)


def _outproj_kernel(o_ref, x_ref, wout_ref, g_ref, b_ref, wr_ref, x1_ref, x1b_ref, aff_ref):
    for r in range(x_ref.shape[0] // OUT_SUB):
        rs = slice(r * OUT_SUB, (r + 1) * OUT_SUB)
        mix = jnp.dot(o_ref[rs, :], wout_ref[...], preferred_element_type=F32)
        x1 = _layer_norm(DN_ALPHA * x_ref[rs, :] + mix, g_ref[...], b_ref[...])
        x1_ref[rs, :] = x1
        hi, lo = _split_bf16(x1)
        x1b_ref[rs, :] = hi
        wr = wr_ref[...]
        la = jnp.dot(hi, wr, preferred_element_type=F32)
        lb = jnp.dot(lo, wr, preferred_element_type=F32)
        logits = (la + pltpu.roll(la, LANES - N_EXPERTS, 1) + lb).T[:N_EXPERTS, :]
        e = jnp.exp(logits - jnp.max(logits, axis=0, keepdims=True))
        aff_ref[:, rs] = e / jnp.sum(e, axis=0, keepdims=True)


def _outproj_call(o_mix, x2d, wout, g, b, wr):
    T = x2d.shape[0]
    tm = TM_OUT
    full = lambda shape: pl.BlockSpec(shape, lambda i: (0,) * len(shape))
    row = lambda w: pl.BlockSpec((tm, w), lambda i: (i, 0))
    return pl.pallas_call(
        _outproj_kernel,
        grid=(T // tm,),
        in_specs=[row(MIX_WIDTH), row(D_MODEL), full(wout.shape), full(g.shape), full(b.shape), full(wr.shape)],
        out_specs=[row(D_MODEL), row(D_MODEL), pl.BlockSpec((N_EXPERTS, tm), lambda i: (0, i))],
        out_shape=[jax.ShapeDtypeStruct((T, D_MODEL), F32), jax.ShapeDtypeStruct((T, D_MODEL), BF16),
                   jax.ShapeDtypeStruct((N_EXPERTS, T), F32)],
        compiler_params=_cparams(1),
        name="outproj_router",
    )(o_mix, x2d, wout, g, b, wr)


def _excl_cumsum(mask, upper):
    n = mask.shape[1]
    ones = jnp.where(mask, 1.0, 0.0)
    run = jnp.zeros((mask.shape[0], 1), F32)
    outs = []
    for j in range(n // LANES):
        blk = ones[:, j * LANES:(j + 1) * LANES]
        outs.append(jnp.dot(blk.astype(BF16), upper, preferred_element_type=F32) + run)
        run = run + jnp.sum(blk, axis=1, keepdims=True)
    return jnp.concatenate(outs, axis=1)


def _topk_kernel(cap, seq, aff_ref, pos_ref, post_ref, tab_ref):
    nb, n_e = pos_ref.shape[0], pos_ref.shape[1]
    aff_of = lambda j: aff_ref[:, j * seq:(j + 1) * seq]
    count = lambda mask: jnp.sum(jnp.where(mask, 1.0, 0.0), axis=1, keepdims=True)

    def step(k, ts):
        bit = jnp.left_shift(jnp.int32(1), 30 - k)
        out = []
        for j in range(nb):
            cand = ts[j] | bit
            cnt = count(aff_of(j) >= lax.bitcast_convert_type(cand, F32))
            out.append(jnp.where(cnt >= cap, cand, ts[j]))
        return tuple(out)

    ts = lax.fori_loop(0, 31, step, tuple(jnp.zeros((n_e, 1), jnp.int32) for _ in range(nb)))
    ri = lax.broadcasted_iota(jnp.int32, (LANES, LANES), 0)
    ci = lax.broadcasted_iota(jnp.int32, (LANES, LANES), 1)
    upper = jnp.where(ri < ci, 1.0, 0.0).astype(BF16)
    lane = lax.broadcasted_iota(jnp.int32, (n_e, LANES), 1)
    for j in range(nb):
        a = aff_of(j)
        above = a >= lax.bitcast_convert_type(ts[j] + 1, F32)
        window = jnp.logical_and(a >= lax.bitcast_convert_type(ts[j], F32), jnp.logical_not(above))

        def more(st):
            return jnp.max(st[1]) > 0.0

        def take_round(st):
            sel_f, rem, win_f = st
            win = win_f > 0.5
            top = jnp.max(jnp.where(win, a, -1.0), axis=1, keepdims=True)
            eq = jnp.logical_and(win, a == top)
            take = jnp.minimum(count(eq), rem)
            pick = jnp.logical_and(eq, _excl_cumsum(eq, upper) < take)
            return (jnp.where(pick, 1.0, sel_f), rem - take, jnp.where(eq, 0.0, win_f))

        sel_f, _, _ = lax.while_loop(more, take_round, (jnp.where(above, 1.0, 0.0), cap - count(above),
                                                         jnp.where(window, 1.0, 0.0)))
        sel = sel_f > 0.5
        pos = jnp.where(sel, _excl_cumsum(sel, upper), -1.0)
        pos_ref[j] = pos
        padded = jnp.concatenate([pos, jnp.full((LANES - n_e, seq), -1.0, F32)], axis=0)
        post_ref[j] = padded.T

        ones = jnp.where(sel, 1.0, 0.0)
        lo = jnp.zeros((n_e, 1), F32)
        tab = jnp.zeros((n_e, LANES), F32)
        over = jnp.zeros((n_e, 1), F32)
        for kb in range(seq // TOK_BLK):
            hi = lo + jnp.sum(ones[:, kb * TOK_BLK:(kb + 1) * TOK_BLK], axis=1, keepdims=True)
            win = jnp.minimum(jnp.floor(lo * (1.0 / WIN_ALIGN)) * WIN_ALIGN, float(cap - WIN))
            over = jnp.maximum(over, jnp.where(hi - win > WIN, 1.0, 0.0))
            tab = jnp.where(lane == kb, win, tab)
            lo = hi
        tab = jnp.where(lane == TAB_FLAG, jnp.max(over, axis=0, keepdims=True), tab)
        tab_ref[j] = tab.astype(jnp.int32)


def _topk_call(aff_t, batch, seq, cap):
    assert seq // TOK_BLK <= TAB_FLAG < TAB_COLS and cap % WIN == 0 and WIN % WIN_ALIGN == 0
    nb = TOPK_SEQS
    return pl.pallas_call(
        functools.partial(_topk_kernel, cap, seq),
        grid=(batch // nb,),
        in_specs=[pl.BlockSpec((N_EXPERTS, nb * seq), lambda b: (0, b))],
        out_specs=[pl.BlockSpec((nb, N_EXPERTS, seq), lambda b: (b, 0, 0)),
                   pl.BlockSpec((nb, seq, LANES), lambda b: (b, 0, 0)),
                   pl.BlockSpec((nb, N_EXPERTS, LANES), lambda b: (b, 0, 0))],
        out_shape=[jax.ShapeDtypeStruct((batch, N_EXPERTS, seq), F32),
                   jax.ShapeDtypeStruct((batch, seq, LANES), F32),
                   jax.ShapeDtypeStruct((batch, N_EXPERTS, LANES), jnp.int32)],
        compiler_params=_cparams(1),
        name="expert_choice",
    )(aff_t)


def _tab_at(tab_ref, b, e, col):
    return tab_ref[(b * N_EXPERTS + e) * TAB_COLS + col]


def _gather_kernel(cap, tab_ref, pos_ref, aff_ref, x_ref, xg_ref, gs_ref):
    b = pl.program_id(0)
    seq = x_ref.shape[0]
    fits = _tab_at(tab_ref, b, 0, TAB_FLAG) == 0

    @pl.when(fits)
    def _():
        xg_ref[...] = jnp.zeros_like(xg_ref)
        gs_ref[...] = jnp.zeros_like(gs_ref)
        wslot = lax.broadcasted_iota(jnp.int32, (WIN, 1), 0)
        for kb in range(seq // TOK_BLK):
            ts = slice(kb * TOK_BLK, (kb + 1) * TOK_BLK)
            wins, hots = [], []
            for e in range(N_EXPERTS):
                win = pl.multiple_of(_tab_at(tab_ref, b, e, kb), WIN_ALIGN)
                hit = pos_ref[0, e:e + 1, ts] == (wslot + win).astype(F32)
                g = jnp.sum(jnp.where(hit, aff_ref[e:e + 1, ts], 0.0), axis=1, keepdims=True)
                gs_ref[e, pl.ds(win, WIN), :] += jnp.broadcast_to(g, (WIN, LANES))
                wins.append(win)
                hots.append(jnp.where(hit, 1.0, 0.0).astype(BF16))
            rows = jnp.dot(jnp.concatenate(hots, axis=0), x_ref[ts, :], preferred_element_type=F32)
            for e in range(N_EXPERTS):
                xg_ref[e, pl.ds(wins[e], WIN), :] += rows[e * WIN:(e + 1) * WIN].astype(BF16)

    @pl.when(jnp.logical_not(fits))
    def _():
        slot = lax.broadcasted_iota(jnp.int32, (cap, seq), 0).astype(F32)

        def body(e, carry):
            onehot = pos_ref[0, pl.ds(e, 1), :] == slot
            xg = jnp.dot(jnp.where(onehot, 1.0, 0.0).astype(BF16), x_ref[...], preferred_element_type=F32)
            xg_ref[e] = xg.astype(BF16)
            g = jnp.sum(jnp.where(onehot, aff_ref[pl.ds(e, 1), :], 0.0), axis=1, keepdims=True)
            gs_ref[e] = jnp.broadcast_to(g, (cap, LANES))
            return carry

        lax.fori_loop(0, N_EXPERTS, body, 0)


def _gather_call(tab, pos, aff_t, x1b, batch, seq, cap):
    grid_spec = pltpu.PrefetchScalarGridSpec(
        num_scalar_prefetch=1,
        grid=(batch,),
        in_specs=[pl.BlockSpec((1, N_EXPERTS, seq), lambda b, tab: (b, 0, 0)),
                  pl.BlockSpec((N_EXPERTS, seq), lambda b, tab: (0, b)),
                  pl.BlockSpec((seq, D_MODEL), lambda b, tab: (b, 0))],
        out_specs=[pl.BlockSpec((N_EXPERTS, cap, D_MODEL), lambda b, tab: (0, b, 0)),
                   pl.BlockSpec((N_EXPERTS, cap, LANES), lambda b, tab: (0, b, 0))],
    )
    return pl.pallas_call(
        functools.partial(_gather_kernel, cap),
        grid_spec=grid_spec,
        out_shape=[jax.ShapeDtypeStruct((N_EXPERTS, batch * cap, D_MODEL), BF16),
                   jax.ShapeDtypeStruct((N_EXPERTS, batch * cap, LANES), F32)],
        compiler_params=_cparams(1),
        name="moe_gather",
    )(tab, pos, aff_t, x1b)


def _ffn_kernel(xg_ref, gs_ref, wg_ref, wu_ref, wd_ref, y_ref, acc_ref, wgb_ref, wub_ref, wdb_ref):
    f = pl.program_id(2)
    wgb_ref[...] = wg_ref[0].astype(BF16)
    wub_ref[...] = wu_ref[0].astype(BF16)
    wdb_ref[...] = wd_ref[0].astype(BF16)

    @pl.when(f == 0)
    def _():
        acc_ref[...] = jnp.zeros_like(acc_ref)

    def body(i, carry):
        r0 = pl.multiple_of(i * FFN_RB, FFN_RB)
        xb = xg_ref[0, pl.ds(r0, FFN_RB), :]
        g = jnp.dot(xb, wgb_ref[...], preferred_element_type=F32)
        u = jnp.dot(xb, wub_ref[...], preferred_element_type=F32)
        h = (g * jax.nn.sigmoid(g) * u).astype(BF16)
        acc_ref[pl.ds(r0, FFN_RB), :] += jnp.dot(h, wdb_ref[...], preferred_element_type=F32)
        return carry

    lax.fori_loop(0, FFN_ROWS // FFN_RB, body, 0, unroll=True)

    @pl.when(f == pl.num_programs(2) - 1)
    def _():
        y_ref[0] = (acc_ref[...] * gs_ref[0][:, :1]).astype(BF16)


def _ffn_call(xg, gs, w_gate, w_up, w_down):
    n_e, rows, _ = xg.shape
    return pl.pallas_call(
        _ffn_kernel,
        grid=(n_e, rows // FFN_ROWS, EXPERT_FF // FF_TILE),
        in_specs=[pl.BlockSpec((1, FFN_ROWS, D_MODEL), lambda e, m, f: (e, m, 0)),
                  pl.BlockSpec((1, FFN_ROWS, LANES), lambda e, m, f: (e, m, 0)),
                  pl.BlockSpec((1, D_MODEL, FF_TILE), lambda e, m, f: (e, 0, f)),
                  pl.BlockSpec((1, D_MODEL, FF_TILE), lambda e, m, f: (e, 0, f)),
                  pl.BlockSpec((1, FF_TILE, D_MODEL), lambda e, m, f: (e, f, 0))],
        out_specs=pl.BlockSpec((1, FFN_ROWS, D_MODEL), lambda e, m, f: (e, m, 0)),
        out_shape=jax.ShapeDtypeStruct((n_e, rows, D_MODEL), BF16),
        scratch_shapes=[pltpu.VMEM((FFN_ROWS, D_MODEL), F32),
                        pltpu.VMEM((D_MODEL, FF_TILE), BF16),
                        pltpu.VMEM((D_MODEL, FF_TILE), BF16),
                        pltpu.VMEM((FF_TILE, D_MODEL), BF16)],
        compiler_params=_cparams(3),
        name="moe_ffn",
    )(xg, gs, w_gate, w_up, w_down)


def _final_kernel(cap, tab_ref, post_ref, yg_ref, x1_ref, p_ref, wpp_ref, wpg_ref, g2_ref, b2_ref, g3_ref, b3_ref,
                  out_ref, moe_ref):
    b = pl.program_id(0)
    ts = moe_ref.shape[0]
    fits = _tab_at(tab_ref, b, 0, TAB_FLAG) == 0
    per_tile = TOK_BLK // WIN

    @pl.when(fits)
    def _():
        lane = lax.broadcasted_iota(jnp.int32, (1, TOK_BLK), 1)
        grp = lane // WIN
        for j in range(ts // TOK_BLK):
            kb = pl.program_id(1) * (ts // TOK_BLK) + j
            post = post_ref[0, j * TOK_BLK:(j + 1) * TOK_BLK, :]
            acc = jnp.zeros((TOK_BLK, D_MODEL), F32)
            for e0 in range(0, N_EXPERTS, per_tile):
                wins = [pl.multiple_of(_tab_at(tab_ref, b, e0 + r, kb), WIN_ALIGN) for r in range(per_tile)]
                rhs = jnp.concatenate([yg_ref[e0 + r, pl.ds(wins[r], WIN), :] for r in range(per_tile)], axis=0)
                tgt = (lane - grp * WIN).astype(F32)
                col = jnp.zeros((TOK_BLK, TOK_BLK), F32)
                for r in range(per_tile):
                    tgt = tgt + jnp.where(grp == r, wins[r].astype(F32), 0.0)
                    col = jnp.where(grp == r, post[:, e0 + r:e0 + r + 1], col)
                onehot = jnp.where(col == tgt, 1.0, 0.0).astype(BF16)
                acc = acc + jnp.dot(onehot, rhs, preferred_element_type=F32)
            moe_ref[j * TOK_BLK:(j + 1) * TOK_BLK, :] = acc

    @pl.when(jnp.logical_not(fits))
    def _():
        slot = lax.broadcasted_iota(jnp.int32, (ts, cap), 1).astype(F32)

        def body(e, moe):
            lane_e = lax.broadcasted_iota(jnp.int32, (1, LANES), 1) == e
            col = jnp.sum(jnp.where(lane_e, post_ref[0], 0.0), axis=1, keepdims=True)
            onehot = jnp.where(col == slot, 1.0, 0.0).astype(BF16)
            return moe + jnp.dot(onehot, yg_ref[e], preferred_element_type=F32)

        moe_ref[...] = lax.fori_loop(0, N_EXPERTS, body, jnp.zeros((ts, D_MODEL), F32))

    x2 = _layer_norm(DN_ALPHA * x1_ref[...] + moe_ref[...], g2_ref[...], b2_ref[...])
    emb = jnp.dot(p_ref[...].astype(BF16), wpp_ref[...], preferred_element_type=F32)
    gate = jax.nn.sigmoid(jnp.dot(x2.astype(BF16), wpg_ref[...], preferred_element_type=F32))
    out_ref[...] = _layer_norm(DN_ALPHA * x2 + gate * emb, g3_ref[...], b3_ref[...])


def _final_call(tab, post, yg, x1, p2d, wpp, wpg, g2, b2, g3, b3, batch, seq, cap):
    T = x1.shape[0]
    ts = TS_FIN
    nt = seq // ts
    full = lambda shape: pl.BlockSpec(shape, lambda b, i, tab: (0,) * len(shape))
    row = lambda w: pl.BlockSpec((ts, w), lambda b, i, tab: (b * nt + i, 0))
    grid_spec = pltpu.PrefetchScalarGridSpec(
        num_scalar_prefetch=1,
        grid=(batch, nt),
        in_specs=[pl.BlockSpec((1, ts, LANES), lambda b, i, tab: (b, i, 0)),
                  pl.BlockSpec((N_EXPERTS, cap, D_MODEL), lambda b, i, tab: (0, b, 0)),
                  row(D_MODEL), row(PLE_DIM), full(wpp.shape), full(wpg.shape),
                  full(g2.shape), full(b2.shape), full(g3.shape), full(b3.shape)],
        out_specs=row(D_MODEL),
        scratch_shapes=[pltpu.VMEM((ts, D_MODEL), F32)],
    )
    return pl.pallas_call(
        functools.partial(_final_kernel, cap),
        grid_spec=grid_spec,
        out_shape=jax.ShapeDtypeStruct((T, D_MODEL), F32),
        compiler_params=_cparams(2),
        name="combine_ple",
    )(tab, post, yg, x1, p2d, wpp, wpg, g2, b2, g3, b3)


def _rope_tables(seq):
    f32 = np.float32
    rows = seq // GRID_W
    row = np.repeat(np.arange(rows, dtype=f32), GRID_W)
    col = np.tile(np.arange(GRID_W, dtype=f32), rows)

    def table(head_dim, lane_of):
        sub = head_dim // 2
        half = sub // 2
        inv_freq = f32(ROPE_BASE) ** (-np.arange(half, dtype=f32) * f32(2.0) / f32(sub))
        ang_row = row[:, None] * inv_freq[None, :]
        ang_col = col[:, None] * inv_freq[None, :]
        ang = np.concatenate([ang_row, ang_col], axis=1)
        idx = np.where(lane_of < 0, 0, (lane_of // sub) * half + (lane_of % sub) % half)
        first = (lane_of % sub) < half
        active = lane_of >= 0
        a = ang[:, idx]
        cos = np.where(active[None, :], np.cos(a), 1.0).astype(f32)
        sin = np.where(active[None, :], np.where(first[None, :], -np.sin(a), np.sin(a)), 0.0).astype(f32)
        return jnp.asarray(cos), jnp.asarray(sin)

    lanes = np.arange(LANES)
    cos64, sin64 = table(HEAD_DIM, lanes % HEAD_DIM)
    mla_lane = np.where((lanes >= MLA_NOPE_DIM) & (lanes < MLA_NOPE_DIM + MLA_ROPE_DIM), lanes - MLA_NOPE_DIM, -1)
    cosm, sinm = table(MLA_ROPE_DIM, mla_lane)
    return cos64, sin64, cosm, sinm


def _block_ones(n, blk):
    i = np.arange(n) // blk
    return jnp.asarray((i[:, None] == i[None, :]).astype(np.float32), dtype=BF16)


def kernel(x, p, w_in, q_norm, k_norm, cq_norm, ckv_norm, w_uq, w_ukv, w_out, ln_attn_g, ln_attn_b, w_router,
           w_gate, w_up, w_down, ln_ffn_g, ln_ffn_b, w_ple_proj, w_ple_gate, ln_ple_g, ln_ple_b):
    assert w_in.shape[0] == DEPTH
    B, S, D = x.shape
    T = B * S
    cap = CAPACITY_FACTOR * S // N_EXPERTS
    x2d = x.reshape(T, D)

    wi = w_in[0]
    zc = lambda n: jnp.zeros((D, n), F32)
    win = jnp.concatenate([wi[:, :OFF_KR], zc(MLA_NOPE_DIM), wi[:, OFF_KR:], zc(LANES - MLA_NOPE_DIM - MLA_ROPE_DIM)],
                          axis=1).astype(BF16)
    qk = MLA_NOPE_DIM + MLA_ROPE_DIM
    wuq = jnp.pad(w_uq[0].reshape(MLA_Q_RANK, MLA_HEADS, qk), ((0, 0), (0, 0), (0, LANES - qk)))
    wuq = wuq.reshape(MLA_Q_RANK, MLA_HEADS * LANES).astype(BF16)
    wkv = w_ukv[0].reshape(MLA_KV_RANK, MLA_HEADS, MLA_NOPE_DIM + MLA_V_DIM)
    wk = jnp.pad(wkv[:, :, :MLA_NOPE_DIM], ((0, 0), (0, 0), (0, LANES - MLA_NOPE_DIM)))
    wukv = jnp.concatenate([wk.reshape(MLA_KV_RANK, MLA_HEADS * LANES),
                            wkv[:, :, MLA_NOPE_DIM:].reshape(MLA_KV_RANK, MLA_HEADS * MLA_V_DIM)], axis=1).astype(BF16)
    qg = (jnp.tile(q_norm[0], GQA_HEADS) * (HEAD_DIM ** -0.5 * LOG2E)).reshape(1, SZ_Q)
    kg = jnp.tile(k_norm[0], GQA_KV_HEADS).reshape(1, SZ_KV)
    cqg = cq_norm[0].reshape(1, MLA_Q_RANK)
    ckvg = ckv_norm[0].reshape(1, MLA_KV_RANK)
    cos64, sin64, cosm, sinm = _rope_tables(S)

    q_all, k_all, v_all = _inproj_call(x2d, win, wuq, wukv, _block_ones(SZ_Q, HEAD_DIM), _block_ones(SZ_KV, HEAD_DIM),
                                       qg, kg, cqg, ckvg, cos64, sin64, cosm, sinm, S)
    o_mix = _attn_call(q_all, k_all, v_all, B, S)

    wrh = w_router[0].astype(BF16)
    wrl = (w_router[0] - wrh.astype(F32)).astype(BF16)
    wr = jnp.concatenate([wrh, wrl, jnp.zeros((D, LANES - 2 * N_EXPERTS), BF16)], axis=1)
    x1, x1b, aff_t = _outproj_call(o_mix, x2d, w_out[0].astype(BF16), ln_attn_g[0].reshape(1, D),
                                   ln_attn_b[0].reshape(1, D), wr)

    pos, post, tab = _topk_call(aff_t, B, S, cap)
    tab = tab[:, :, :TAB_COLS].reshape(-1)
    xg, gs = _gather_call(tab, pos, aff_t, x1b, B, S, cap)
    yg = _ffn_call(xg, gs, w_gate[0], w_up[0], w_down[0])
    out = _final_call(tab, post, yg, x1, p[0].reshape(T, PLE_DIM), w_ple_proj[0].astype(BF16), w_ple_gate[0].astype(BF16),
                      ln_ffn_g[0].reshape(1, D), ln_ffn_b[0].reshape(1, D), ln_ple_g[0].reshape(1, D),
                      ln_ple_b[0].reshape(1, D), B, S, cap)
    return out.reshape(B, S, D)
```

```python
import functools

import numpy as np
import jax
import jax.numpy as jnp
from jax import lax
from jax.experimental import pallas as pl
from jax.experimental.pallas import tpu as pltpu

F32 = jnp.float32
BF16 = jnp.bfloat16

D_MODEL = 1024
HEAD_DIM = 64
GQA_HEADS = 8
GQA_KV_HEADS = 2
MLA_HEADS = 8
MLA_NOPE_DIM = 64
MLA_ROPE_DIM = 32
MLA_V_DIM = 64
MLA_Q_RANK = 256
MLA_KV_RANK = 256
N_EXPERTS = 16
CAPACITY_FACTOR = 2
EXPERT_FF = 2816
PLE_DIM = 256
GRID_W = 64
ROPE_BASE = 10000.0
LN_EPS = 1e-5
RMS_EPS = 1e-6
DEPTH = 1
DN_ALPHA = (2 * DEPTH) ** 0.25
LOG2E = 1.4426950408889634

LANES = 128
SZ_Q = GQA_HEADS * HEAD_DIM
SZ_KV = GQA_KV_HEADS * HEAD_DIM
IN_COLS = SZ_Q + 2 * SZ_KV + MLA_Q_RANK + MLA_KV_RANK + LANES
OFF_K = SZ_Q
OFF_V = OFF_K + SZ_KV
OFF_CQ = OFF_V + SZ_KV
OFF_CKV = OFF_CQ + MLA_Q_RANK
OFF_KR = OFF_CKV + MLA_KV_RANK
N_CHUNKS = 12
N_PAIRS = 8
QKV_COLS = N_CHUNKS * LANES
MIX_WIDTH = N_PAIRS * LANES

TM_IN = 1024
IN_SUB = 256
TQ = 256
PAIRS_PER_STEP = 2
TM_OUT = 1024
OUT_SUB = 1024
FF_TILE = 256
FFN_ROWS = 2048
FFN_RB = 1024
TS_FIN = 512
TOPK_SEQS = 8
TOK_BLK = 256
WIN = 64
WIN_ALIGN = 16
TAB_COLS = 16
TAB_FLAG = 8

VMEM_LIMIT = 56 * 1024 * 1024


def _cparams(n_axes, vmem=VMEM_LIMIT):
    return pltpu.CompilerParams(dimension_semantics=("arbitrary",) * n_axes, vmem_limit_bytes=vmem)


def _layer_norm(y, g, b):
    mu = jnp.mean(y, axis=-1, keepdims=True)
    yc = y - mu
    var = jnp.mean(yc * yc, axis=-1, keepdims=True)
    return yc * lax.rsqrt(var + LN_EPS) * g + b


def _split_bf16(v):
    hi = v.astype(BF16)
    lo = (v - hi.astype(F32)).astype(BF16)
    return hi, lo


def _rope(v, cos, sin, dist):
    lane = lax.broadcasted_iota(jnp.int32, v.shape, 1)
    first = (lane % (2 * dist)) < dist
    rot = jnp.where(first, pltpu.roll(v, LANES - dist, 1), pltpu.roll(v, dist, 1))
    return v * cos + rot * sin


def _head_sumsq(v, g_ref):
    hi, lo = _split_bf16(v * v)
    g = g_ref[...]
    return jnp.dot(hi, g, preferred_element_type=F32) + jnp.dot(lo, g, preferred_element_type=F32)


def _inproj_kernel(x_ref, win_ref, wuq_ref, wukv_ref, g512_ref, g128_ref, qg_ref, kg_ref, cqg_ref, ckvg_ref,
                   cos64_ref, sin64_ref, cosm_ref, sinm_ref, q_out, k_out, v_out):
    for r in range(x_ref.shape[0] // IN_SUB):
        _inproj_rows(slice(r * IN_SUB, (r + 1) * IN_SUB), x_ref, win_ref, wuq_ref, wukv_ref, g512_ref, g128_ref,
                     qg_ref, kg_ref, cqg_ref, ckvg_ref, cos64_ref, sin64_ref, cosm_ref, sinm_ref, q_out, k_out, v_out)


def _inproj_rows(rs, x_ref, win_ref, wuq_ref, wukv_ref, g512_ref, g128_ref, qg_ref, kg_ref, cqg_ref, ckvg_ref,
                 cos64_ref, sin64_ref, cosm_ref, sinm_ref, q_out, k_out, v_out):
    proj = jnp.dot(x_ref[rs, :].astype(BF16), win_ref[...], preferred_element_type=F32)
    cos64, sin64 = cos64_ref[rs, :], sin64_ref[rs, :]
    cosm, sinm = cosm_ref[rs, :], sinm_ref[rs, :]
    lane = lax.broadcasted_iota(jnp.int32, (proj.shape[0], LANES), 1)
    low = lane < HEAD_DIM

    qg = proj[:, :SZ_Q]
    qn = qg * lax.rsqrt(_head_sumsq(qg, g512_ref) * (1.0 / HEAD_DIM) + RMS_EPS) * qg_ref[...]
    for c in range(SZ_Q // LANES):
        sl = slice(c * LANES, (c + 1) * LANES)
        q_out[rs, sl] = _rope(qn[:, sl], cos64, sin64, HEAD_DIM // 4).astype(BF16)

    kg = proj[:, OFF_K:OFF_K + SZ_KV]
    kn = kg * lax.rsqrt(_head_sumsq(kg, g128_ref) * (1.0 / HEAD_DIM) + RMS_EPS) * kg_ref[...]
    kr = _rope(kn, cos64, sin64, HEAD_DIM // 4)
    vg = proj[:, OFF_V:OFF_V + SZ_KV]
    one_lo = jnp.where(lane == HEAD_DIM, 1.0, 0.0)
    one_hi = jnp.where(lane == 0, 1.0, 0.0)
    for src, dst, fill_lo, fill_hi in ((kr, k_out, 0.0, 0.0), (vg, v_out, one_lo, one_hi)):
        lo = jnp.where(low, src, fill_lo)
        hi = jnp.where(low, fill_hi, src)
        dst[rs, 0 * LANES:1 * LANES] = lo.astype(BF16)
        dst[rs, 1 * LANES:2 * LANES] = pltpu.roll(lo, HEAD_DIM, 1).astype(BF16)
        dst[rs, 2 * LANES:3 * LANES] = pltpu.roll(hi, HEAD_DIM, 1).astype(BF16)
        dst[rs, 3 * LANES:4 * LANES] = hi.astype(BF16)

    cq = proj[:, OFF_CQ:OFF_CQ + MLA_Q_RANK]
    cqn = cq * lax.rsqrt(jnp.sum(cq * cq, axis=-1, keepdims=True) * (1.0 / MLA_Q_RANK) + RMS_EPS) * cqg_ref[...]
    qm = jnp.dot(cqn.astype(BF16), wuq_ref[...], preferred_element_type=F32)
    mla_scale = (MLA_NOPE_DIM + MLA_ROPE_DIM) ** -0.5 * LOG2E
    for h in range(MLA_HEADS):
        sl = slice(h * LANES, (h + 1) * LANES)
        dsl = slice(SZ_Q + h * LANES, SZ_Q + (h + 1) * LANES)
        q_out[rs, dsl] = (_rope(qm[:, sl], cosm, sinm, MLA_ROPE_DIM // 4) * mla_scale).astype(BF16)

    ckv = proj[:, OFF_CKV:OFF_CKV + MLA_KV_RANK]
    ckvn = ckv * lax.rsqrt(jnp.sum(ckv * ckv, axis=-1, keepdims=True) * (1.0 / MLA_KV_RANK) + RMS_EPS) * ckvg_ref[...]
    kv = jnp.dot(ckvn.astype(BF16), wukv_ref[...], preferred_element_type=F32)
    kpe = _rope(proj[:, OFF_KR:OFF_KR + LANES], cosm, sinm, MLA_ROPE_DIM // 4)
    for h in range(MLA_HEADS):
        sl = slice(h * LANES, (h + 1) * LANES)
        dsl = slice(SZ_Q + h * LANES, SZ_Q + (h + 1) * LANES)
        k_out[rs, dsl] = (kv[:, sl] + kpe).astype(BF16)
    voff = MLA_HEADS * LANES
    for j in range(MLA_HEADS // 2):
        vp = kv[:, voff + j * LANES: voff + (j + 1) * LANES]
        base = SZ_Q + 2 * j * LANES
        v_out[rs, base:base + LANES] = jnp.where(low, vp, one_lo).astype(BF16)
        v_out[rs, base + LANES:base + 2 * LANES] = jnp.where(low, one_hi, vp).astype(BF16)


def _inproj_call(x2d, win, wuq, wukv, g512, g128, qg, kg, cqg, ckvg, cos64, sin64, cosm, sinm, seq):
    T = x2d.shape[0]
    tm = TM_IN
    tiles_per_seq = seq // tm
    full = lambda shape: pl.BlockSpec(shape, lambda i: (0,) * len(shape))
    tab = pl.BlockSpec((tm, LANES), lambda i: (i % tiles_per_seq, 0))
    out_spec = pl.BlockSpec((tm, QKV_COLS), lambda i: (i, 0))
    out_shape = jax.ShapeDtypeStruct((T, QKV_COLS), BF16)
    return pl.pallas_call(
        _inproj_kernel,
        grid=(T // tm,),
        in_specs=[pl.BlockSpec((tm, D_MODEL), lambda i: (i, 0)),
                  full(win.shape), full(wuq.shape), full(wukv.shape), full(g512.shape), full(g128.shape),
                  full(qg.shape), full(kg.shape), full(cqg.shape), full(ckvg.shape),
                  tab, tab, tab, tab],
        out_specs=[out_spec, out_spec, out_spec],
        out_shape=[out_shape, out_shape, out_shape],
        compiler_params=_cparams(1),
        name="inproj",
    )(x2d, win, wuq, wukv, g512, g128, qg, kg, cqg, ckvg, cos64, sin64, cosm, sinm)


def _attn_kernel(*refs):
    o_ref = refs[-1]
    seq = o_ref.shape[0]

    def head(q, k_ref, v_ref):
        hk = seq // 2
        nt = (((1,), (1,)), ((), ()))
        s0 = lax.dot_general(q, k_ref[:hk, :], nt, preferred_element_type=F32)
        s1 = lax.dot_general(q, k_ref[hk:, :], nt, preferred_element_type=F32)
        m = jnp.maximum(jnp.max(s0, axis=-1, keepdims=True), jnp.max(s1, axis=-1, keepdims=True))
        p0 = jnp.exp2(s0 - m).astype(BF16)
        p1 = jnp.exp2(s1 - m).astype(BF16)
        return (jnp.dot(p0, v_ref[:hk, :], preferred_element_type=F32)
                + jnp.dot(p1, v_ref[hk:, :], preferred_element_type=F32))

    for j in range(PAIRS_PER_STEP):
        qe_ref, qo_ref, ke_ref, ko_ref, v_ref = refs[5 * j:5 * j + 5]

        def body(i, carry):
            r0 = pl.multiple_of(i * TQ, TQ)
            oe = head(qe_ref[pl.ds(r0, TQ), :], ke_ref, v_ref)[:, :LANES]
            oo = head(qo_ref[pl.ds(r0, TQ), :], ko_ref, v_ref)[:, LANES:]
            low = lax.broadcasted_iota(jnp.int32, (TQ, LANES), 1) < HEAD_DIM
            o = jnp.where(low, oe / oe[:, HEAD_DIM:HEAD_DIM + 1], oo / oo[:, 0:1])
            o_ref[pl.ds(r0, TQ), j * LANES:(j + 1) * LANES] = o.astype(BF16)
            return carry

        lax.fori_loop(0, seq // TQ, body, 0, unroll=True)


def _q_chunk(pr, odd):
    return jnp.where(pr < 4, pr, 4 + 2 * (pr - 4) + odd)


def _kv_chunk(pr, odd):
    return jnp.where(pr < 4, 2 * (pr // 2) + odd, 4 + 2 * (pr - 4) + odd)


def _attn_call(q_all, k_all, v_all, batch, seq):
    T = q_all.shape[0]
    blk = (seq, LANES)
    pps = PAIRS_PER_STEP
    in_specs, operands = [], []
    for j in range(pps):
        pair = lambda g, j=j: g * pps + j
        for odd in (0, 1):
            in_specs.append(pl.BlockSpec(blk, lambda b, g, odd=odd, pair=pair: (b, _q_chunk(pair(g), odd))))
            operands.append(q_all)
        for odd in (0, 1):
            in_specs.append(pl.BlockSpec(blk, lambda b, g, odd=odd, pair=pair: (b, _kv_chunk(pair(g), odd))))
            operands.append(k_all)
        in_specs.append(pl.BlockSpec((seq, 2 * LANES), lambda b, g, pair=pair: (b, _kv_chunk(pair(g), 0) // 2)))
        operands.append(v_all)
    return pl.pallas_call(
        _attn_kernel,
        grid=(batch, N_PAIRS // pps),
        in_specs=in_specs,
        out_specs=pl.BlockSpec((seq, pps * LANES), lambda b, g: (b, g)),
        out_shape=jax.ShapeDtypeStruct((T, MIX_WIDTH), BF16),
        compiler_params=_cparams(2),
        name="attention",
    )(*operands)


def _outproj_kernel(o_ref, x_ref, wout_ref, g_ref, b_ref, wr_ref, x1_ref, x1b_ref, aff_ref):
    for r in range(x_ref.shape[0] // OUT_SUB):
        rs = slice(r * OUT_SUB, (r + 1) * OUT_SUB)
        mix = jnp.dot(o_ref[rs, :], wout_ref[...], preferred_element_type=F32)
        x1 = _layer_norm(DN_ALPHA * x_ref[rs, :] + mix, g_ref[...], b_ref[...])
        x1_ref[rs, :] = x1
        hi, lo = _split_bf16(x1)
        x1b_ref[rs, :] = hi
        wr = wr_ref[...]
        la = jnp.dot(hi, wr, preferred_element_type=F32)
        lb = jnp.dot(lo, wr, preferred_element_type=F32)
        logits = (la + pltpu.roll(la, LANES - N_EXPERTS, 1) + lb).T[:N_EXPERTS, :]
        e = jnp.exp(logits - jnp.max(logits, axis=0, keepdims=True))
        aff_ref[:, rs] = e / jnp.sum(e, axis=0, keepdims=True)


def _outproj_call(o_mix, x2d, wout, g, b, wr):
    T = x2d.shape[0]
    tm = TM_OUT
    full = lambda shape: pl.BlockSpec(shape, lambda i: (0,) * len(shape))
    row = lambda w: pl.BlockSpec((tm, w), lambda i: (i, 0))
    return pl.pallas_call(
        _outproj_kernel,
        grid=(T // tm,),
        in_specs=[row(MIX_WIDTH), row(D_MODEL), full(wout.shape), full(g.shape), full(b.shape), full(wr.shape)],
        out_specs=[row(D_MODEL), row(D_MODEL), pl.BlockSpec((N_EXPERTS, tm), lambda i: (0, i))],
        out_shape=[jax.ShapeDtypeStruct((T, D_MODEL), F32), jax.ShapeDtypeStruct((T, D_MODEL), BF16),
                   jax.ShapeDtypeStruct((N_EXPERTS, T), F32)],
        compiler_params=_cparams(1),
        name="outproj_router",
    )(o_mix, x2d, wout, g, b, wr)


def _excl_cumsum(mask, upper):
    n = mask.shape[1]
    ones = jnp.where(mask, 1.0, 0.0)
    run = jnp.zeros((mask.shape[0], 1), F32)
    outs = []
    for j in range(n // LANES):
        blk = ones[:, j * LANES:(j + 1) * LANES]
        outs.append(jnp.dot(blk.astype(BF16), upper, preferred_element_type=F32) + run)
        run = run + jnp.sum(blk, axis=1, keepdims=True)
    return jnp.concatenate(outs, axis=1)


def _topk_kernel(cap, seq, aff_ref, pos_ref, post_ref, tab_ref):
    nb, n_e = pos_ref.shape[0], pos_ref.shape[1]
    aff_of = lambda j: aff_ref[:, j * seq:(j + 1) * seq]
    count = lambda mask: jnp.sum(jnp.where(mask, 1.0, 0.0), axis=1, keepdims=True)

    def step(k, ts):
        bit = jnp.left_shift(jnp.int32(1), 30 - k)
        out = []
        for j in range(nb):
            cand = ts[j] | bit
            cnt = count(aff_of(j) >= lax.bitcast_convert_type(cand, F32))
            out.append(jnp.where(cnt >= cap, cand, ts[j]))
        return tuple(out)

    ts = lax.fori_loop(0, 31, step, tuple(jnp.zeros((n_e, 1), jnp.int32) for _ in range(nb)))
    ri = lax.broadcasted_iota(jnp.int32, (LANES, LANES), 0)
    ci = lax.broadcasted_iota(jnp.int32, (LANES, LANES), 1)
    upper = jnp.where(ri < ci, 1.0, 0.0).astype(BF16)
    lane = lax.broadcasted_iota(jnp.int32, (n_e, LANES), 1)
    for j in range(nb):
        a = aff_of(j)
        above = a >= lax.bitcast_convert_type(ts[j] + 1, F32)
        window = jnp.logical_and(a >= lax.bitcast_convert_type(ts[j], F32), jnp.logical_not(above))

        def more(st):
            return jnp.max(st[1]) > 0.0

        def take_round(st):
            sel_f, rem, win_f = st
            win = win_f > 0.5
            top = jnp.max(jnp.where(win, a, -1.0), axis=1, keepdims=True)
            eq = jnp.logical_and(win, a == top)
            take = jnp.minimum(count(eq), rem)
            pick = jnp.logical_and(eq, _excl_cumsum(eq, upper) < take)
            return (jnp.where(pick, 1.0, sel_f), rem - take, jnp.where(eq, 0.0, win_f))

        sel_f, _, _ = lax.while_loop(more, take_round, (jnp.where(above, 1.0, 0.0), cap - count(above),
                                                         jnp.where(window, 1.0, 0.0)))
        sel = sel_f > 0.5
        pos = jnp.where(sel, _excl_cumsum(sel, upper), -1.0)
        pos_ref[j] = pos
        padded = jnp.concatenate([pos, jnp.full((LANES - n_e, seq), -1.0, F32)], axis=0)
        post_ref[j] = padded.T

        ones = jnp.where(sel, 1.0, 0.0)
        lo = jnp.zeros((n_e, 1), F32)
        tab = jnp.zeros((n_e, LANES), F32)
        over = jnp.zeros((n_e, 1), F32)
        for kb in range(seq // TOK_BLK):
            hi = lo + jnp.sum(ones[:, kb * TOK_BLK:(kb + 1) * TOK_BLK], axis=1, keepdims=True)
            win = jnp.minimum(jnp.floor(lo * (1.0 / WIN_ALIGN)) * WIN_ALIGN, float(cap - WIN))
            over = jnp.maximum(over, jnp.where(hi - win > WIN, 1.0, 0.0))
            tab = jnp.where(lane == kb, win, tab)
            lo = hi
        tab = jnp.where(lane == TAB_FLAG, jnp.max(over, axis=0, keepdims=True), tab)
        tab_ref[j] = tab.astype(jnp.int32)


def _topk_call(aff_t, batch, seq, cap):
    assert seq // TOK_BLK <= TAB_FLAG < TAB_COLS and cap % WIN == 0 and WIN % WIN_ALIGN == 0
    nb = TOPK_SEQS
    return pl.pallas_call(
        functools.partial(_topk_kernel, cap, seq),
        grid=(batch // nb,),
        in_specs=[pl.BlockSpec((N_EXPERTS, nb * seq), lambda b: (0, b))],
        out_specs=[pl.BlockSpec((nb, N_EXPERTS, seq), lambda b: (b, 0, 0)),
                   pl.BlockSpec((nb, seq, LANES), lambda b: (b, 0, 0)),
                   pl.BlockSpec((nb, N_EXPERTS, LANES), lambda b: (b, 0, 0))],
        out_shape=[jax.ShapeDtypeStruct((batch, N_EXPERTS, seq), F32),
                   jax.ShapeDtypeStruct((batch, seq, LANES), F32),
                   jax.ShapeDtypeStruct((batch, N_EXPERTS, LANES), jnp.int32)],
        compiler_params=_cparams(1),
        name="expert_choice",
    )(aff_t)


def _tab_at(tab_ref, b, e, col):
    return tab_ref[(b * N_EXPERTS + e) * TAB_COLS + col]


def _gather_kernel(cap, tab_ref, pos_ref, aff_ref, x_ref, xg_ref, gs_ref):
    b = pl.program_id(0)
    seq = x_ref.shape[0]
    fits = _tab_at(tab_ref, b, 0, TAB_FLAG) == 0

    @pl.when(fits)
    def _():
        xg_ref[...] = jnp.zeros_like(xg_ref)
        gs_ref[...] = jnp.zeros_like(gs_ref)
        wslot = lax.broadcasted_iota(jnp.int32, (WIN, 1), 0)
        for kb in range(seq // TOK_BLK):
            ts = slice(kb * TOK_BLK, (kb + 1) * TOK_BLK)
            wins, hots = [], []
            for e in range(N_EXPERTS):
                win = pl.multiple_of(_tab_at(tab_ref, b, e, kb), WIN_ALIGN)
                hit = pos_ref[0, e:e + 1, ts] == (wslot + win).astype(F32)
                g = jnp.sum(jnp.where(hit, aff_ref[e:e + 1, ts], 0.0), axis=1, keepdims=True)
                gs_ref[e, pl.ds(win, WIN), :] += jnp.broadcast_to(g, (WIN, LANES))
                wins.append(win)
                hots.append(jnp.where(hit, 1.0, 0.0).astype(BF16))
            rows = jnp.dot(jnp.concatenate(hots, axis=0), x_ref[ts, :], preferred_element_type=F32)
            for e in range(N_EXPERTS):
                xg_ref[e, pl.ds(wins[e], WIN), :] += rows[e * WIN:(e + 1) * WIN].astype(BF16)

    @pl.when(jnp.logical_not(fits))
    def _():
        slot = lax.broadcasted_iota(jnp.int32, (cap, seq), 0).astype(F32)

        def body(e, carry):
            onehot = pos_ref[0, pl.ds(e, 1), :] == slot
            xg = jnp.dot(jnp.where(onehot, 1.0, 0.0).astype(BF16), x_ref[...], preferred_element_type=F32)
            xg_ref[e] = xg.astype(BF16)
            g = jnp.sum(jnp.where(onehot, aff_ref[pl.ds(e, 1), :], 0.0), axis=1, keepdims=True)
            gs_ref[e] = jnp.broadcast_to(g, (cap, LANES))
            return carry

        lax.fori_loop(0, N_EXPERTS, body, 0)


def _gather_call(tab, pos, aff_t, x1b, batch, seq, cap):
    grid_spec = pltpu.PrefetchScalarGridSpec(
        num_scalar_prefetch=1,
        grid=(batch,),
        in_specs=[pl.BlockSpec((1, N_EXPERTS, seq), lambda b, tab: (b, 0, 0)),
                  pl.BlockSpec((N_EXPERTS, seq), lambda b, tab: (0, b)),
                  pl.BlockSpec((seq, D_MODEL), lambda b, tab: (b, 0))],
        out_specs=[pl.BlockSpec((N_EXPERTS, cap, D_MODEL), lambda b, tab: (0, b, 0)),
                   pl.BlockSpec((N_EXPERTS, cap, LANES), lambda b, tab: (0, b, 0))],
    )
    return pl.pallas_call(
        functools.partial(_gather_kernel, cap),
        grid_spec=grid_spec,
        out_shape=[jax.ShapeDtypeStruct((N_EXPERTS, batch * cap, D_MODEL), BF16),
                   jax.ShapeDtypeStruct((N_EXPERTS, batch * cap, LANES), F32)],
        compiler_params=_cparams(1),
        name="moe_gather",
    )(tab, pos, aff_t, x1b)


def _ffn_kernel(xg_ref, gs_ref, wg_ref, wu_ref, wd_ref, y_ref, acc_ref, wgb_ref, wub_ref, wdb_ref):
    f = pl.program_id(2)
    wgb_ref[...] = wg_ref[0].astype(BF16)
    wub_ref[...] = wu_ref[0].astype(BF16)
    wdb_ref[...] = wd_ref[0].astype(BF16)

    last = pl.num_programs(2) - 1

    def run(where):
        def body(i, carry):
            rows = pl.ds(pl.multiple_of(i * FFN_RB, FFN_RB), FFN_RB)
            xb = xg_ref[0, rows, :]
            g = jnp.dot(xb, wgb_ref[...], preferred_element_type=F32)
            u = jnp.dot(xb, wub_ref[...], preferred_element_type=F32)
            h = (g * jax.nn.sigmoid(g) * u).astype(BF16)
            y = jnp.dot(h, wdb_ref[...], preferred_element_type=F32)
            if where == "first":
                acc_ref[rows, :] = y
            elif where == "middle":
                acc_ref[rows, :] += y
            else:
                y_ref[0, rows, :] = ((acc_ref[rows, :] + y) * gs_ref[0, rows, :1]).astype(BF16)
            return carry

        lax.fori_loop(0, FFN_ROWS // FFN_RB, body, 0, unroll=True)

    pl.when(f == 0)(functools.partial(run, "first"))
    pl.when(jnp.logical_and(f != 0, f != last))(functools.partial(run, "middle"))
    pl.when(f == last)(functools.partial(run, "last"))


def _ffn_call(xg, gs, w_gate, w_up, w_down):
    n_e, rows, _ = xg.shape
    return pl.pallas_call(
        _ffn_kernel,
        grid=(n_e, rows // FFN_ROWS, EXPERT_FF // FF_TILE),
        in_specs=[pl.BlockSpec((1, FFN_ROWS, D_MODEL), lambda e, m, f: (e, m, 0)),
                  pl.BlockSpec((1, FFN_ROWS, LANES), lambda e, m, f: (e, m, 0)),
                  pl.BlockSpec((1, D_MODEL, FF_TILE), lambda e, m, f: (e, 0, f)),
                  pl.BlockSpec((1, D_MODEL, FF_TILE), lambda e, m, f: (e, 0, f)),
                  pl.BlockSpec((1, FF_TILE, D_MODEL), lambda e, m, f: (e, f, 0))],
        out_specs=pl.BlockSpec((1, FFN_ROWS, D_MODEL), lambda e, m, f: (e, m, 0)),
        out_shape=jax.ShapeDtypeStruct((n_e, rows, D_MODEL), BF16),
        scratch_shapes=[pltpu.VMEM((FFN_ROWS, D_MODEL), F32),
                        pltpu.VMEM((D_MODEL, FF_TILE), BF16),
                        pltpu.VMEM((D_MODEL, FF_TILE), BF16),
                        pltpu.VMEM((FF_TILE, D_MODEL), BF16)],
        compiler_params=_cparams(3),
        name="moe_ffn",
    )(xg, gs, w_gate, w_up, w_down)


def _final_kernel(cap, tab_ref, post_ref, yg_ref, x1_ref, p_ref, wpp_ref, wpg_ref, g2_ref, b2_ref, g3_ref, b3_ref,
                  out_ref, moe_ref):
    b = pl.program_id(0)
    ts = moe_ref.shape[0]
    fits = _tab_at(tab_ref, b, 0, TAB_FLAG) == 0
    per_tile = TOK_BLK // WIN

    @pl.when(fits)
    def _():
        lane = lax.broadcasted_iota(jnp.int32, (1, TOK_BLK), 1)
        grp = lane // WIN
        for j in range(ts // TOK_BLK):
            kb = pl.program_id(1) * (ts // TOK_BLK) + j
            post = post_ref[0, j * TOK_BLK:(j + 1) * TOK_BLK, :]
            acc = jnp.zeros((TOK_BLK, D_MODEL), F32)
            for e0 in range(0, N_EXPERTS, per_tile):
                wins = [pl.multiple_of(_tab_at(tab_ref, b, e0 + r, kb), WIN_ALIGN) for r in range(per_tile)]
                rhs = jnp.concatenate([yg_ref[e0 + r, pl.ds(wins[r], WIN), :] for r in range(per_tile)], axis=0)
                tgt = (lane - grp * WIN).astype(F32)
                col = jnp.zeros((TOK_BLK, TOK_BLK), F32)
                for r in range(per_tile):
                    tgt = tgt + jnp.where(grp == r, wins[r].astype(F32), 0.0)
                    col = jnp.where(grp == r, post[:, e0 + r:e0 + r + 1], col)
                onehot = jnp.where(col == tgt, 1.0, 0.0).astype(BF16)
                acc = acc + jnp.dot(onehot, rhs, preferred_element_type=F32)
            moe_ref[j * TOK_BLK:(j + 1) * TOK_BLK, :] = acc

    @pl.when(jnp.logical_not(fits))
    def _():
        slot = lax.broadcasted_iota(jnp.int32, (ts, cap), 1).astype(F32)

        def body(e, moe):
            lane_e = lax.broadcasted_iota(jnp.int32, (1, LANES), 1) == e
            col = jnp.sum(jnp.where(lane_e, post_ref[0], 0.0), axis=1, keepdims=True)
            onehot = jnp.where(col == slot, 1.0, 0.0).astype(BF16)
            return moe + jnp.dot(onehot, yg_ref[e], preferred_element_type=F32)

        moe_ref[...] = lax.fori_loop(0, N_EXPERTS, body, jnp.zeros((ts, D_MODEL), F32))

    x2 = _layer_norm(DN_ALPHA * x1_ref[...] + moe_ref[...], g2_ref[...], b2_ref[...])
    emb = jnp.dot(p_ref[...].astype(BF16), wpp_ref[...], preferred_element_type=F32)
    gate = jax.nn.sigmoid(jnp.dot(x2.astype(BF16), wpg_ref[...], preferred_element_type=F32))
    out_ref[...] = _layer_norm(DN_ALPHA * x2 + gate * emb, g3_ref[...], b3_ref[...])


def _final_call(tab, post, yg, x1, p2d, wpp, wpg, g2, b2, g3, b3, batch, seq, cap):
    T = x1.shape[0]
    ts = TS_FIN
    nt = seq // ts
    full = lambda shape: pl.BlockSpec(shape, lambda b, i, tab: (0,) * len(shape))
    row = lambda w: pl.BlockSpec((ts, w), lambda b, i, tab: (b * nt + i, 0))
    grid_spec = pltpu.PrefetchScalarGridSpec(
        num_scalar_prefetch=1,
        grid=(batch, nt),
        in_specs=[pl.BlockSpec((1, ts, LANES), lambda b, i, tab: (b, i, 0)),
                  pl.BlockSpec((N_EXPERTS, cap, D_MODEL), lambda b, i, tab: (0, b, 0)),
                  row(D_MODEL), row(PLE_DIM), full(wpp.shape), full(wpg.shape),
                  full(g2.shape), full(b2.shape), full(g3.shape), full(b3.shape)],
        out_specs=row(D_MODEL),
        scratch_shapes=[pltpu.VMEM((ts, D_MODEL), F32)],
    )
    return pl.pallas_call(
        functools.partial(_final_kernel, cap),
        grid_spec=grid_spec,
        out_shape=jax.ShapeDtypeStruct((T, D_MODEL), F32),
        compiler_params=_cparams(2),
        name="combine_ple",
    )(tab, post, yg, x1, p2d, wpp, wpg, g2, b2, g3, b3)


def _rope_tables(seq):
    f32 = np.float32
    rows = seq // GRID_W
    row = np.repeat(np.arange(rows, dtype=f32), GRID_W)
    col = np.tile(np.arange(GRID_W, dtype=f32), rows)

    def table(head_dim, lane_of):
        sub = head_dim // 2
        half = sub // 2
        inv_freq = f32(ROPE_BASE) ** (-np.arange(half, dtype=f32) * f32(2.0) / f32(sub))
        ang_row = row[:, None] * inv_freq[None, :]
        ang_col = col[:, None] * inv_freq[None, :]
        ang = np.concatenate([ang_row, ang_col], axis=1)
        idx = np.where(lane_of < 0, 0, (lane_of // sub) * half + (lane_of % sub) % half)
        first = (lane_of % sub) < half
        active = lane_of >= 0
        a = ang[:, idx]
        cos = np.where(active[None, :], np.cos(a), 1.0).astype(f32)
        sin = np.where(active[None, :], np.where(first[None, :], -np.sin(a), np.sin(a)), 0.0).astype(f32)
        return jnp.asarray(cos), jnp.asarray(sin)

    lanes = np.arange(LANES)
    cos64, sin64 = table(HEAD_DIM, lanes % HEAD_DIM)
    mla_lane = np.where((lanes >= MLA_NOPE_DIM) & (lanes < MLA_NOPE_DIM + MLA_ROPE_DIM), lanes - MLA_NOPE_DIM, -1)
    cosm, sinm = table(MLA_ROPE_DIM, mla_lane)
    return cos64, sin64, cosm, sinm


def _block_ones(n, blk):
    i = np.arange(n) // blk
    return jnp.asarray((i[:, None] == i[None, :]).astype(np.float32), dtype=BF16)


def kernel(x, p, w_in, q_norm, k_norm, cq_norm, ckv_norm, w_uq, w_ukv, w_out, ln_attn_g, ln_attn_b, w_router,
           w_gate, w_up, w_down, ln_ffn_g, ln_ffn_b, w_ple_proj, w_ple_gate, ln_ple_g, ln_ple_b):
    assert w_in.shape[0] == DEPTH
    B, S, D = x.shape
    T = B * S
    cap = CAPACITY_FACTOR * S // N_EXPERTS
    x2d = x.reshape(T, D)

    wi = w_in[0]
    zc = lambda n: jnp.zeros((D, n), F32)
    win = jnp.concatenate([wi[:, :OFF_KR], zc(MLA_NOPE_DIM), wi[:, OFF_KR:], zc(LANES - MLA_NOPE_DIM - MLA_ROPE_DIM)],
                          axis=1).astype(BF16)
    qk = MLA_NOPE_DIM + MLA_ROPE_DIM
    wuq = jnp.pad(w_uq[0].reshape(MLA_Q_RANK, MLA_HEADS, qk), ((0, 0), (0, 0), (0, LANES - qk)))
    wuq = wuq.reshape(MLA_Q_RANK, MLA_HEADS * LANES).astype(BF16)
    wkv = w_ukv[0].reshape(MLA_KV_RANK, MLA_HEADS, MLA_NOPE_DIM + MLA_V_DIM)
    wk = jnp.pad(wkv[:, :, :MLA_NOPE_DIM], ((0, 0), (0, 0), (0, LANES - MLA_NOPE_DIM)))
    wukv = jnp.concatenate([wk.reshape(MLA_KV_RANK, MLA_HEADS * LANES),
                            wkv[:, :, MLA_NOPE_DIM:].reshape(MLA_KV_RANK, MLA_HEADS * MLA_V_DIM)], axis=1).astype(BF16)
    qg = (jnp.tile(q_norm[0], GQA_HEADS) * (HEAD_DIM ** -0.5 * LOG2E)).reshape(1, SZ_Q)
    kg = jnp.tile(k_norm[0], GQA_KV_HEADS).reshape(1, SZ_KV)
    cqg = cq_norm[0].reshape(1, MLA_Q_RANK)
    ckvg = ckv_norm[0].reshape(1, MLA_KV_RANK)
    cos64, sin64, cosm, sinm = _rope_tables(S)

    q_all, k_all, v_all = _inproj_call(x2d, win, wuq, wukv, _block_ones(SZ_Q, HEAD_DIM), _block_ones(SZ_KV, HEAD_DIM),
                                       qg, kg, cqg, ckvg, cos64, sin64, cosm, sinm, S)
    o_mix = _attn_call(q_all, k_all, v_all, B, S)

    wrh = w_router[0].astype(BF16)
    wrl = (w_router[0] - wrh.astype(F32)).astype(BF16)
    wr = jnp.concatenate([wrh, wrl, jnp.zeros((D, LANES - 2 * N_EXPERTS), BF16)], axis=1)
    x1, x1b, aff_t = _outproj_call(o_mix, x2d, w_out[0].astype(BF16), ln_attn_g[0].reshape(1, D),
                                   ln_attn_b[0].reshape(1, D), wr)

    pos, post, tab = _topk_call(aff_t, B, S, cap)
    tab = tab[:, :, :TAB_COLS].reshape(-1)
    xg, gs = _gather_call(tab, pos, aff_t, x1b, B, S, cap)
    yg = _ffn_call(xg, gs, w_gate[0], w_up[0], w_down[0])
    out = _final_call(tab, post, yg, x1, p[0].reshape(T, PLE_DIM), w_ple_proj[0].astype(BF16), w_ple_gate[0].astype(BF16),
                      ln_ffn_g[0].reshape(1, D), ln_ffn_b[0].reshape(1, D), ln_ple_g[0].reshape(1, D),
                      ln_ple_b[0].reshape(1, D), B, S, cap)
    return out.reshape(B, S, D)
```

```python
import functools

import numpy as np
import jax
import jax.numpy as jnp
from jax import lax
from jax.experimental import pallas as pl
from jax.experimental.pallas import tpu as pltpu

F32 = jnp.float32
BF16 = jnp.bfloat16

D_MODEL = 1024
HEAD_DIM = 64
GQA_HEADS = 8
GQA_KV_HEADS = 2
MLA_HEADS = 8
MLA_NOPE_DIM = 64
MLA_ROPE_DIM = 32
MLA_V_DIM = 64
MLA_Q_RANK = 256
MLA_KV_RANK = 256
N_EXPERTS = 16
CAPACITY_FACTOR = 2
EXPERT_FF = 2816
PLE_DIM = 256
GRID_W = 64
ROPE_BASE = 10000.0
LN_EPS = 1e-5
RMS_EPS = 1e-6
DEPTH = 1
DN_ALPHA = (2 * DEPTH) ** 0.25
LOG2E = 1.4426950408889634

LANES = 128
SZ_Q = GQA_HEADS * HEAD_DIM
SZ_KV = GQA_KV_HEADS * HEAD_DIM
IN_COLS = SZ_Q + 2 * SZ_KV + MLA_Q_RANK + MLA_KV_RANK + LANES
OFF_K = SZ_Q
OFF_V = OFF_K + SZ_KV
OFF_CQ = OFF_V + SZ_KV
OFF_CKV = OFF_CQ + MLA_Q_RANK
OFF_KR = OFF_CKV + MLA_KV_RANK
N_CHUNKS = 12
N_PAIRS = 8
QKV_COLS = N_CHUNKS * LANES
MIX_WIDTH = N_PAIRS * LANES

TM_IN = 1024
IN_SUB = 256
TQ = 256
PAIRS_PER_STEP = 2
ATTN_AHEAD = 2
TM_OUT = 1024
OUT_SUB = 1024
FF_TILE = 256
FFN_ROWS = 2048
FFN_RB = 1024
TS_FIN = 512
TOPK_SEQS = 8
TOK_BLK = 256
WIN = 64
WIN_ALIGN = 16
TAB_COLS = 16
TAB_FLAG = 8

VMEM_LIMIT = 56 * 1024 * 1024


def _cparams(n_axes, vmem=VMEM_LIMIT):
    return pltpu.CompilerParams(dimension_semantics=("arbitrary",) * n_axes, vmem_limit_bytes=vmem)


def _layer_norm(y, g, b):
    mu = jnp.mean(y, axis=-1, keepdims=True)
    yc = y - mu
    var = jnp.mean(yc * yc, axis=-1, keepdims=True)
    return yc * lax.rsqrt(var + LN_EPS) * g + b


def _split_bf16(v):
    hi = v.astype(BF16)
    lo = (v - hi.astype(F32)).astype(BF16)
    return hi, lo


def _rope(v, cos, sin, dist):
    lane = lax.broadcasted_iota(jnp.int32, v.shape, 1)
    first = (lane % (2 * dist)) < dist
    rot = jnp.where(first, pltpu.roll(v, LANES - dist, 1), pltpu.roll(v, dist, 1))
    return v * cos + rot * sin


def _head_sumsq(v, g_ref):
    hi, lo = _split_bf16(v * v)
    g = g_ref[...]
    return jnp.dot(hi, g, preferred_element_type=F32) + jnp.dot(lo, g, preferred_element_type=F32)


def _inproj_kernel(x_ref, win_ref, wuq_ref, wukv_ref, g512_ref, g128_ref, qg_ref, kg_ref, cqg_ref, ckvg_ref,
                   cos64_ref, sin64_ref, cosm_ref, sinm_ref, q_out, k_out, v_out):
    for r in range(x_ref.shape[0] // IN_SUB):
        _inproj_rows(slice(r * IN_SUB, (r + 1) * IN_SUB), x_ref, win_ref, wuq_ref, wukv_ref, g512_ref, g128_ref,
                     qg_ref, kg_ref, cqg_ref, ckvg_ref, cos64_ref, sin64_ref, cosm_ref, sinm_ref, q_out, k_out, v_out)


def _inproj_rows(rs, x_ref, win_ref, wuq_ref, wukv_ref, g512_ref, g128_ref, qg_ref, kg_ref, cqg_ref, ckvg_ref,
                 cos64_ref, sin64_ref, cosm_ref, sinm_ref, q_out, k_out, v_out):
    proj = jnp.dot(x_ref[rs, :].astype(BF16), win_ref[...], preferred_element_type=F32)
    cos64, sin64 = cos64_ref[rs, :], sin64_ref[rs, :]
    cosm, sinm = cosm_ref[rs, :], sinm_ref[rs, :]
    lane = lax.broadcasted_iota(jnp.int32, (proj.shape[0], LANES), 1)
    low = lane < HEAD_DIM

    qg = proj[:, :SZ_Q]
    qn = qg * lax.rsqrt(_head_sumsq(qg, g512_ref) * (1.0 / HEAD_DIM) + RMS_EPS) * qg_ref[...]
    for c in range(SZ_Q // LANES):
        sl = slice(c * LANES, (c + 1) * LANES)
        q_out[rs, sl] = _rope(qn[:, sl], cos64, sin64, HEAD_DIM // 4).astype(BF16)

    kg = proj[:, OFF_K:OFF_K + SZ_KV]
    kn = kg * lax.rsqrt(_head_sumsq(kg, g128_ref) * (1.0 / HEAD_DIM) + RMS_EPS) * kg_ref[...]
    kr = _rope(kn, cos64, sin64, HEAD_DIM // 4)
    vg = proj[:, OFF_V:OFF_V + SZ_KV]
    one_lo = jnp.where(lane == HEAD_DIM, 1.0, 0.0)
    one_hi = jnp.where(lane == 0, 1.0, 0.0)
    for src, dst, fill_lo, fill_hi in ((kr, k_out, 0.0, 0.0), (vg, v_out, one_lo, one_hi)):
        lo = jnp.where(low, src, fill_lo)
        hi = jnp.where(low, fill_hi, src)
        dst[rs, 0 * LANES:1 * LANES] = lo.astype(BF16)
        dst[rs, 1 * LANES:2 * LANES] = pltpu.roll(lo, HEAD_DIM, 1).astype(BF16)
        dst[rs, 2 * LANES:3 * LANES] = pltpu.roll(hi, HEAD_DIM, 1).astype(BF16)
        dst[rs, 3 * LANES:4 * LANES] = hi.astype(BF16)

    cq = proj[:, OFF_CQ:OFF_CQ + MLA_Q_RANK]
    cqn = cq * lax.rsqrt(jnp.sum(cq * cq, axis=-1, keepdims=True) * (1.0 / MLA_Q_RANK) + RMS_EPS) * cqg_ref[...]
    qm = jnp.dot(cqn.astype(BF16), wuq_ref[...], preferred_element_type=F32)
    mla_scale = (MLA_NOPE_DIM + MLA_ROPE_DIM) ** -0.5 * LOG2E
    for h in range(MLA_HEADS):
        sl = slice(h * LANES, (h + 1) * LANES)
        dsl = slice(SZ_Q + h * LANES, SZ_Q + (h + 1) * LANES)
        q_out[rs, dsl] = (_rope(qm[:, sl], cosm, sinm, MLA_ROPE_DIM // 4) * mla_scale).astype(BF16)

    ckv = proj[:, OFF_CKV:OFF_CKV + MLA_KV_RANK]
    ckvn = ckv * lax.rsqrt(jnp.sum(ckv * ckv, axis=-1, keepdims=True) * (1.0 / MLA_KV_RANK) + RMS_EPS) * ckvg_ref[...]
    kv = jnp.dot(ckvn.astype(BF16), wukv_ref[...], preferred_element_type=F32)
    kpe = _rope(proj[:, OFF_KR:OFF_KR + LANES], cosm, sinm, MLA_ROPE_DIM // 4)
    for h in range(MLA_HEADS):
        sl = slice(h * LANES, (h + 1) * LANES)
        dsl = slice(SZ_Q + h * LANES, SZ_Q + (h + 1) * LANES)
        k_out[rs, dsl] = (kv[:, sl] + kpe).astype(BF16)
    voff = MLA_HEADS * LANES
    for j in range(MLA_HEADS // 2):
        vp = kv[:, voff + j * LANES: voff + (j + 1) * LANES]
        base = SZ_Q + 2 * j * LANES
        v_out[rs, base:base + LANES] = jnp.where(low, vp, one_lo).astype(BF16)
        v_out[rs, base + LANES:base + 2 * LANES] = jnp.where(low, one_hi, vp).astype(BF16)


def _inproj_call(x2d, win, wuq, wukv, g512, g128, qg, kg, cqg, ckvg, cos64, sin64, cosm, sinm, seq):
    T = x2d.shape[0]
    tm = TM_IN
    tiles_per_seq = seq // tm
    full = lambda shape: pl.BlockSpec(shape, lambda i: (0,) * len(shape))
    tab = pl.BlockSpec((tm, LANES), lambda i: (i % tiles_per_seq, 0))
    out_spec = pl.BlockSpec((tm, QKV_COLS), lambda i: (i, 0))
    out_shape = jax.ShapeDtypeStruct((T, QKV_COLS), BF16)
    return pl.pallas_call(
        _inproj_kernel,
        grid=(T // tm,),
        in_specs=[pl.BlockSpec((tm, D_MODEL), lambda i: (i, 0)),
                  full(win.shape), full(wuq.shape), full(wukv.shape), full(g512.shape), full(g128.shape),
                  full(qg.shape), full(kg.shape), full(cqg.shape), full(ckvg.shape),
                  tab, tab, tab, tab],
        out_specs=[out_spec, out_spec, out_spec],
        out_shape=[out_shape, out_shape, out_shape],
        compiler_params=_cparams(1),
        name="inproj",
    )(x2d, win, wuq, wukv, g512, g128, qg, kg, cqg, ckvg, cos64, sin64, cosm, sinm)


def _attn_kernel(*refs):
    o_ref = refs[-1]
    seq = o_ref.shape[0]

    hk = seq // 2

    def probs(q, k_ref):
        nt = (((1,), (1,)), ((), ()))
        s0 = lax.dot_general(q, k_ref[:hk, :], nt, preferred_element_type=F32)
        s1 = lax.dot_general(q, k_ref[hk:, :], nt, preferred_element_type=F32)
        m = jnp.maximum(jnp.max(s0, axis=-1, keepdims=True), jnp.max(s1, axis=-1, keepdims=True))
        return jnp.exp2(s0 - m).astype(BF16), jnp.exp2(s1 - m).astype(BF16)

    def values(p, v_ref):
        return (jnp.dot(p[0], v_ref[:hk, :], preferred_element_type=F32)
                + jnp.dot(p[1], v_ref[hk:, :], preferred_element_type=F32))

    low =lax.broadcasted_iota(jnp.int32, (TQ, LANES), 1) < HEAD_DIM
    even_out = {}

    def finish(p, v_ref, j, i, odd):
        o = values(p, v_ref)
        if not odd:
            even_out[(j, i)] = o[:, :LANES]
            return
        oe, oo = even_out.pop((j, i)), o[:, LANES:]
        out = jnp.where(low, oe / oe[:, HEAD_DIM:HEAD_DIM + 1], oo / oo[:, 0:1])
        o_ref[i * TQ:(i + 1) * TQ, j * LANES:(j + 1) * LANES] = out.astype(BF16)

    pending = []
    for j in range(PAIRS_PER_STEP):
        qe_ref, qo_ref, ke_ref, ko_ref, v_ref = refs[5 * j:5 * j + 5]
        for i in range(seq // TQ):
            for odd, q_ref, k_ref in ((0, qe_ref, ke_ref), (1, qo_ref, ko_ref)):
                p = probs(q_ref[i * TQ:(i + 1) * TQ, :], k_ref)
                if len(pending) == ATTN_AHEAD:
                    finish(*pending.pop(0))
                pending.append((p, v_ref, j, i, odd))
    for item in pending:
        finish(*item)


def _q_chunk(pr, odd):
    return jnp.where(pr < 4, pr, 4 + 2 * (pr - 4) + odd)


def _kv_chunk(pr, odd):
    return jnp.where(pr < 4, 2 * (pr // 2) + odd, 4 + 2 * (pr - 4) + odd)


def _attn_call(q_all, k_all, v_all, batch, seq):
    T = q_all.shape[0]
    blk = (seq, LANES)
    pps = PAIRS_PER_STEP
    in_specs, operands = [], []
    for j in range(pps):
        pair = lambda g, j=j: g * pps + j
        for odd in (0, 1):
            in_specs.append(pl.BlockSpec(blk, lambda b, g, odd=odd, pair=pair: (b, _q_chunk(pair(g), odd))))
            operands.append(q_all)
        for odd in (0, 1):
            in_specs.append(pl.BlockSpec(blk, lambda b, g, odd=odd, pair=pair: (b, _kv_chunk(pair(g), odd))))
            operands.append(k_all)
        in_specs.append(pl.BlockSpec((seq, 2 * LANES), lambda b, g, pair=pair: (b, _kv_chunk(pair(g), 0) // 2)))
        operands.append(v_all)
    return pl.pallas_call(
        _attn_kernel,
        grid=(batch, N_PAIRS // pps),
        in_specs=in_specs,
        out_specs=pl.BlockSpec((seq, pps * LANES), lambda b, g: (b, g)),
        out_shape=jax.ShapeDtypeStruct((T, MIX_WIDTH), BF16),
        compiler_params=_cparams(2),
        name="attention",
    )(*operands)


def _outproj_kernel(o_ref, x_ref, wout_ref, g_ref, b_ref, wr_ref, x1_ref, x1b_ref, aff_ref):
    for r in range(x_ref.shape[0] // OUT_SUB):
        rs = slice(r * OUT_SUB, (r + 1) * OUT_SUB)
        mix = jnp.dot(o_ref[rs, :], wout_ref[...], preferred_element_type=F32)
        x1 = _layer_norm(DN_ALPHA * x_ref[rs, :] + mix, g_ref[...], b_ref[...])
        x1_ref[rs, :] = x1
        hi, lo = _split_bf16(x1)
        x1b_ref[rs, :] = hi
        wr = wr_ref[...]
        la = jnp.dot(hi, wr, preferred_element_type=F32)
        lb = jnp.dot(lo, wr, preferred_element_type=F32)
        logits = (la + pltpu.roll(la, LANES - N_EXPERTS, 1) + lb).T[:N_EXPERTS, :]
        e = jnp.exp(logits - jnp.max(logits, axis=0, keepdims=True))
        aff_ref[:, rs] = e / jnp.sum(e, axis=0, keepdims=True)


def _outproj_call(o_mix, x2d, wout, g, b, wr):
    T = x2d.shape[0]
    tm = TM_OUT
    full = lambda shape: pl.BlockSpec(shape, lambda i: (0,) * len(shape))
    row = lambda w: pl.BlockSpec((tm, w), lambda i: (i, 0))
    return pl.pallas_call(
        _outproj_kernel,
        grid=(T // tm,),
        in_specs=[row(MIX_WIDTH), row(D_MODEL), full(wout.shape), full(g.shape), full(b.shape), full(wr.shape)],
        out_specs=[row(D_MODEL), row(D_MODEL), pl.BlockSpec((N_EXPERTS, tm), lambda i: (0, i))],
        out_shape=[jax.ShapeDtypeStruct((T, D_MODEL), F32), jax.ShapeDtypeStruct((T, D_MODEL), BF16),
                   jax.ShapeDtypeStruct((N_EXPERTS, T), F32)],
        compiler_params=_cparams(1),
        name="outproj_router",
    )(o_mix, x2d, wout, g, b, wr)


def _excl_cumsum(mask, upper):
    n = mask.shape[1]
    ones = jnp.where(mask, 1.0, 0.0)
    run = jnp.zeros((mask.shape[0], 1), F32)
    outs = []
    for j in range(n // LANES):
        blk = ones[:, j * LANES:(j + 1) * LANES]
        outs.append(jnp.dot(blk.astype(BF16), upper, preferred_element_type=F32) + run)
        run = run + jnp.sum(blk, axis=1, keepdims=True)
    return jnp.concatenate(outs, axis=1)


def _topk_kernel(cap, seq, aff_ref, pos_ref, post_ref, tab_ref):
    nb, n_e = pos_ref.shape[0], pos_ref.shape[1]
    aff_of = lambda j: aff_ref[:, j * seq:(j + 1) * seq]
    count = lambda mask: jnp.sum(jnp.where(mask, 1.0, 0.0), axis=1, keepdims=True)

    def step(k, ts):
        bit = jnp.left_shift(jnp.int32(1), 30 - k)
        out = []
        for j in range(nb):
            cand = ts[j] | bit
            cnt = count(aff_of(j) >= lax.bitcast_convert_type(cand, F32))
            out.append(jnp.where(cnt >= cap, cand, ts[j]))
        return tuple(out)

    ts = lax.fori_loop(0, 31, step, tuple(jnp.zeros((n_e, 1), jnp.int32) for _ in range(nb)))
    ri = lax.broadcasted_iota(jnp.int32, (LANES, LANES), 0)
    ci = lax.broadcasted_iota(jnp.int32, (LANES, LANES), 1)
    upper = jnp.where(ri < ci, 1.0, 0.0).astype(BF16)
    lane = lax.broadcasted_iota(jnp.int32, (n_e, LANES), 1)
    for j in range(nb):
        a = aff_of(j)
        above = a >= lax.bitcast_convert_type(ts[j] + 1, F32)
        window = jnp.logical_and(a >= lax.bitcast_convert_type(ts[j], F32), jnp.logical_not(above))

        def more(st):
            return jnp.max(st[1]) > 0.0

        def take_round(st):
            sel_f, rem, win_f = st
            win = win_f > 0.5
            top = jnp.max(jnp.where(win, a, -1.0), axis=1, keepdims=True)
            eq = jnp.logical_and(win, a == top)
            take = jnp.minimum(count(eq), rem)
            pick = jnp.logical_and(eq, _excl_cumsum(eq, upper) < take)
            return (jnp.where(pick, 1.0, sel_f), rem - take, jnp.where(eq, 0.0, win_f))

        sel_f, _, _ = lax.while_loop(more, take_round, (jnp.where(above, 1.0, 0.0), cap - count(above),
                                                         jnp.where(window, 1.0, 0.0)))
        sel = sel_f > 0.5
        pos = jnp.where(sel, _excl_cumsum(sel, upper), -1.0)
        pos_ref[j] = pos
        padded = jnp.concatenate([pos, jnp.full((LANES - n_e, seq), -1.0, F32)], axis=0)
        post_ref[j] = padded.T

        ones = jnp.where(sel, 1.0, 0.0)
        lo = jnp.zeros((n_e, 1), F32)
        tab = jnp.zeros((n_e, LANES), F32)
        over = jnp.zeros((n_e, 1), F32)
        for kb in range(seq // TOK_BLK):
            hi = lo + jnp.sum(ones[:, kb * TOK_BLK:(kb + 1) * TOK_BLK], axis=1, keepdims=True)
            win = jnp.minimum(jnp.floor(lo * (1.0 / WIN_ALIGN)) * WIN_ALIGN, float(cap - WIN))
            over = jnp.maximum(over, jnp.where(hi - win > WIN, 1.0, 0.0))
            tab = jnp.where(lane == kb, win, tab)
            lo = hi
        tab = jnp.where(lane == TAB_FLAG, jnp.max(over, axis=0, keepdims=True), tab)
        tab_ref[j] = tab.astype(jnp.int32)


def _topk_call(aff_t, batch, seq, cap):
    assert seq // TOK_BLK <= TAB_FLAG < TAB_COLS and cap % WIN == 0 and WIN % WIN_ALIGN == 0
    nb = TOPK_SEQS
    return pl.pallas_call(
        functools.partial(_topk_kernel, cap, seq),
        grid=(batch // nb,),
        in_specs=[pl.BlockSpec((N_EXPERTS, nb * seq), lambda b: (0, b))],
        out_specs=[pl.BlockSpec((nb, N_EXPERTS, seq), lambda b: (b, 0, 0)),
                   pl.BlockSpec((nb, seq, LANES), lambda b: (b, 0, 0)),
                   pl.BlockSpec((nb, N_EXPERTS, LANES), lambda b: (b, 0, 0))],
        out_shape=[jax.ShapeDtypeStruct((batch, N_EXPERTS, seq), F32),
                   jax.ShapeDtypeStruct((batch, seq, LANES), F32),
                   jax.ShapeDtypeStruct((batch, N_EXPERTS, LANES), jnp.int32)],
        compiler_params=_cparams(1),
        name="expert_choice",
    )(aff_t)


def _tab_at(tab_ref, b, e, col):
    return tab_ref[(b * N_EXPERTS + e) * TAB_COLS + col]


def _gather_kernel(cap, tab_ref, pos_ref, aff_ref, x_ref, xg_ref, gs_ref):
    b = pl.program_id(0)
    seq = x_ref.shape[0]
    fits = _tab_at(tab_ref, b, 0, TAB_FLAG) == 0

    @pl.when(fits)
    def _():
        xg_ref[...] = jnp.zeros_like(xg_ref)
        gs_ref[...] = jnp.zeros_like(gs_ref)
        wslot = lax.broadcasted_iota(jnp.int32, (WIN, 1), 0)
        for kb in range(seq // TOK_BLK):
            ts = slice(kb * TOK_BLK, (kb + 1) * TOK_BLK)
            wins, hots = [], []
            for e in range(N_EXPERTS):
                win = pl.multiple_of(_tab_at(tab_ref, b, e, kb), WIN_ALIGN)
                hit = pos_ref[0, e:e + 1, ts] == (wslot + win).astype(F32)
                g = jnp.sum(jnp.where(hit, aff_ref[e:e + 1, ts], 0.0), axis=1, keepdims=True)
                gs_ref[e, pl.ds(win, WIN), :] += jnp.broadcast_to(g, (WIN, LANES))
                wins.append(win)
                hots.append(jnp.where(hit, 1.0, 0.0).astype(BF16))
            rows = jnp.dot(jnp.concatenate(hots, axis=0), x_ref[ts, :], preferred_element_type=F32)
            for e in range(N_EXPERTS):
                xg_ref[e, pl.ds(wins[e], WIN), :] += rows[e * WIN:(e + 1) * WIN].astype(BF16)

    @pl.when(jnp.logical_not(fits))
    def _():
        slot = lax.broadcasted_iota(jnp.int32, (cap, seq), 0).astype(F32)

        def body(e, carry):
            onehot = pos_ref[0, pl.ds(e, 1), :] == slot
            xg = jnp.dot(jnp.where(onehot, 1.0, 0.0).astype(BF16), x_ref[...], preferred_element_type=F32)
            xg_ref[e] = xg.astype(BF16)
            g = jnp.sum(jnp.where(onehot, aff_ref[pl.ds(e, 1), :], 0.0), axis=1, keepdims=True)
            gs_ref[e] = jnp.broadcast_to(g, (cap, LANES))
            return carry

        lax.fori_loop(0, N_EXPERTS, body, 0)


def _gather_call(tab, pos, aff_t, x1b, batch, seq, cap):
    grid_spec = pltpu.PrefetchScalarGridSpec(
        num_scalar_prefetch=1,
        grid=(batch,),
        in_specs=[pl.BlockSpec((1, N_EXPERTS, seq), lambda b, tab: (b, 0, 0)),
                  pl.BlockSpec((N_EXPERTS, seq), lambda b, tab: (0, b)),
                  pl.BlockSpec((seq, D_MODEL), lambda b, tab: (b, 0))],
        out_specs=[pl.BlockSpec((N_EXPERTS, cap, D_MODEL), lambda b, tab: (0, b, 0)),
                   pl.BlockSpec((N_EXPERTS, cap, LANES), lambda b, tab: (0, b, 0))],
    )
    return pl.pallas_call(
        functools.partial(_gather_kernel, cap),
        grid_spec=grid_spec,
        out_shape=[jax.ShapeDtypeStruct((N_EXPERTS, batch * cap, D_MODEL), BF16),
                   jax.ShapeDtypeStruct((N_EXPERTS, batch * cap, LANES), F32)],
        compiler_params=_cparams(1),
        name="moe_gather",
    )(tab, pos, aff_t, x1b)


def _ffn_kernel(xg_ref, gs_ref, wg_ref, wu_ref, wd_ref, y_ref, acc_ref, wgb_ref, wub_ref, wdb_ref):
    f = pl.program_id(2)
    wgb_ref[...] = wg_ref[0].astype(BF16)
    wub_ref[...] = wu_ref[0].astype(BF16)
    wdb_ref[...] = wd_ref[0].astype(BF16)

    last = pl.num_programs(2) - 1

    def run(where):
        def body(i, carry):
            rows = pl.ds(pl.multiple_of(i * FFN_RB, FFN_RB), FFN_RB)
            xb = xg_ref[0, rows, :]
            g = jnp.dot(xb, wgb_ref[...], preferred_element_type=F32)
            u = jnp.dot(xb, wub_ref[...], preferred_element_type=F32)
            h = (g * jax.nn.sigmoid(g) * u).astype(BF16)
            y = jnp.dot(h, wdb_ref[...], preferred_element_type=F32)
            if where == "first":
                acc_ref[rows, :] = y
            elif where == "middle":
                acc_ref[rows, :] += y
            else:
                y_ref[0, rows, :] = ((acc_ref[rows, :] + y) * gs_ref[0, rows, :1]).astype(BF16)
            return carry

        lax.fori_loop(0, FFN_ROWS // FFN_RB, body, 0, unroll=True)

    pl.when(f == 0)(functools.partial(run, "first"))
    pl.when(jnp.logical_and(f != 0, f != last))(functools.partial(run, "middle"))
    pl.when(f == last)(functools.partial(run, "last"))


def _ffn_call(xg, gs, w_gate, w_up, w_down):
    n_e, rows, _ = xg.shape
    return pl.pallas_call(
        _ffn_kernel,
        grid=(n_e, rows // FFN_ROWS, EXPERT_FF // FF_TILE),
        in_specs=[pl.BlockSpec((1, FFN_ROWS, D_MODEL), lambda e, m, f: (e, m, 0)),
                  pl.BlockSpec((1, FFN_ROWS, LANES), lambda e, m, f: (e, m, 0)),
                  pl.BlockSpec((1, D_MODEL, FF_TILE), lambda e, m, f: (e, 0, f)),
                  pl.BlockSpec((1, D_MODEL, FF_TILE), lambda e, m, f: (e, 0, f)),
                  pl.BlockSpec((1, FF_TILE, D_MODEL), lambda e, m, f: (e, f, 0))],
        out_specs=pl.BlockSpec((1, FFN_ROWS, D_MODEL), lambda e, m, f: (e, m, 0)),
        out_shape=jax.ShapeDtypeStruct((n_e, rows, D_MODEL), BF16),
        scratch_shapes=[pltpu.VMEM((FFN_ROWS, D_MODEL), F32),
                        pltpu.VMEM((D_MODEL, FF_TILE), BF16),
                        pltpu.VMEM((D_MODEL, FF_TILE), BF16),
                        pltpu.VMEM((FF_TILE, D_MODEL), BF16)],
        compiler_params=_cparams(3),
        name="moe_ffn",
    )(xg, gs, w_gate, w_up, w_down)


def _final_kernel(cap, tab_ref, post_ref, yg_ref, x1_ref, p_ref, wpp_ref, wpg_ref, g2_ref, b2_ref, g3_ref, b3_ref,
                  out_ref, moe_ref):
    b = pl.program_id(0)
    ts = moe_ref.shape[0]
    fits = _tab_at(tab_ref, b, 0, TAB_FLAG) == 0
    per_tile = TOK_BLK // WIN

    @pl.when(fits)
    def _():
        lane = lax.broadcasted_iota(jnp.int32, (1, TOK_BLK), 1)
        grp = lane // WIN
        for j in range(ts // TOK_BLK):
            kb = pl.program_id(1) * (ts // TOK_BLK) + j
            post = post_ref[0, j * TOK_BLK:(j + 1) * TOK_BLK, :]
            acc = jnp.zeros((TOK_BLK, D_MODEL), F32)
            for e0 in range(0, N_EXPERTS, per_tile):
                wins = [pl.multiple_of(_tab_at(tab_ref, b, e0 + r, kb), WIN_ALIGN) for r in range(per_tile)]
                rhs = jnp.concatenate([yg_ref[e0 + r, pl.ds(wins[r], WIN), :] for r in range(per_tile)], axis=0)
                tgt = (lane - grp * WIN).astype(F32)
                col = jnp.zeros((TOK_BLK, TOK_BLK), F32)
                for r in range(per_tile):
                    tgt = tgt + jnp.where(grp == r, wins[r].astype(F32), 0.0)
                    col = jnp.where(grp == r, post[:, e0 + r:e0 + r + 1], col)
                onehot = jnp.where(col == tgt, 1.0, 0.0).astype(BF16)
                acc = acc + jnp.dot(onehot, rhs, preferred_element_type=F32)
            moe_ref[j * TOK_BLK:(j + 1) * TOK_BLK, :] = acc

    @pl.when(jnp.logical_not(fits))
    def _():
        slot = lax.broadcasted_iota(jnp.int32, (ts, cap), 1).astype(F32)

        def body(e, moe):
            lane_e = lax.broadcasted_iota(jnp.int32, (1, LANES), 1) == e
            col = jnp.sum(jnp.where(lane_e, post_ref[0], 0.0), axis=1, keepdims=True)
            onehot = jnp.where(col == slot, 1.0, 0.0).astype(BF16)
            return moe + jnp.dot(onehot, yg_ref[e], preferred_element_type=F32)

        moe_ref[...] = lax.fori_loop(0, N_EXPERTS, body, jnp.zeros((ts, D_MODEL), F32))

    x2 = _layer_norm(DN_ALPHA * x1_ref[...] + moe_ref[...], g2_ref[...], b2_ref[...])
    emb = jnp.dot(p_ref[...].astype(BF16), wpp_ref[...], preferred_element_type=F32)
    gate = jax.nn.sigmoid(jnp.dot(x2.astype(BF16), wpg_ref[...], preferred_element_type=F32))
    out_ref[...] = _layer_norm(DN_ALPHA * x2 + gate * emb, g3_ref[...], b3_ref[...])


def _final_call(tab, post, yg, x1, p2d, wpp, wpg, g2, b2, g3, b3, batch, seq, cap):
    T = x1.shape[0]
    ts = TS_FIN
    nt = seq // ts
    full = lambda shape: pl.BlockSpec(shape, lambda b, i, tab: (0,) * len(shape))
    row = lambda w: pl.BlockSpec((ts, w), lambda b, i, tab: (b * nt + i, 0))
    grid_spec = pltpu.PrefetchScalarGridSpec(
        num_scalar_prefetch=1,
        grid=(batch, nt),
        in_specs=[pl.BlockSpec((1, ts, LANES), lambda b, i, tab: (b, i, 0)),
                  pl.BlockSpec((N_EXPERTS, cap, D_MODEL), lambda b, i, tab: (0, b, 0)),
                  row(D_MODEL), row(PLE_DIM), full(wpp.shape), full(wpg.shape),
                  full(g2.shape), full(b2.shape), full(g3.shape), full(b3.shape)],
        out_specs=row(D_MODEL),
        scratch_shapes=[pltpu.VMEM((ts, D_MODEL), F32)],
    )
    return pl.pallas_call(
        functools.partial(_final_kernel, cap),
        grid_spec=grid_spec,
        out_shape=jax.ShapeDtypeStruct((T, D_MODEL), F32),
        compiler_params=_cparams(2),
        name="combine_ple",
    )(tab, post, yg, x1, p2d, wpp, wpg, g2, b2, g3, b3)


def _rope_tables(seq):
    f32 = np.float32
    rows = seq // GRID_W
    row = np.repeat(np.arange(rows, dtype=f32), GRID_W)
    col = np.tile(np.arange(GRID_W, dtype=f32), rows)

    def table(head_dim, lane_of):
        sub = head_dim // 2
        half = sub // 2
        inv_freq = f32(ROPE_BASE) ** (-np.arange(half, dtype=f32) * f32(2.0) / f32(sub))
        ang_row = row[:, None] * inv_freq[None, :]
        ang_col = col[:, None] * inv_freq[None, :]
        ang = np.concatenate([ang_row, ang_col], axis=1)
        idx = np.where(lane_of < 0, 0, (lane_of // sub) * half + (lane_of % sub) % half)
        first = (lane_of % sub) < half
        active = lane_of >= 0
        a = ang[:, idx]
        cos = np.where(active[None, :], np.cos(a), 1.0).astype(f32)
        sin = np.where(active[None, :], np.where(first[None, :], -np.sin(a), np.sin(a)), 0.0).astype(f32)
        return jnp.asarray(cos), jnp.asarray(sin)

    lanes = np.arange(LANES)
    cos64, sin64 = table(HEAD_DIM, lanes % HEAD_DIM)
    mla_lane = np.where((lanes >= MLA_NOPE_DIM) & (lanes < MLA_NOPE_DIM + MLA_ROPE_DIM), lanes - MLA_NOPE_DIM, -1)
    cosm, sinm = table(MLA_ROPE_DIM, mla_lane)
    return cos64, sin64, cosm, sinm


def _block_ones(n, blk):
    i = np.arange(n) // blk
    return jnp.asarray((i[:, None] == i[None, :]).astype(np.float32), dtype=BF16)


def kernel(x, p, w_in, q_norm, k_norm, cq_norm, ckv_norm, w_uq, w_ukv, w_out, ln_attn_g, ln_attn_b, w_router,
           w_gate, w_up, w_down, ln_ffn_g, ln_ffn_b, w_ple_proj, w_ple_gate, ln_ple_g, ln_ple_b):
    assert w_in.shape[0] == DEPTH
    B, S, D = x.shape
    T = B * S
    cap = CAPACITY_FACTOR * S // N_EXPERTS
    x2d = x.reshape(T, D)

    wi = w_in[0]
    zc = lambda n: jnp.zeros((D, n), F32)
    win = jnp.concatenate([wi[:, :OFF_KR], zc(MLA_NOPE_DIM), wi[:, OFF_KR:], zc(LANES - MLA_NOPE_DIM - MLA_ROPE_DIM)],
                          axis=1).astype(BF16)
    qk = MLA_NOPE_DIM + MLA_ROPE_DIM
    wuq = jnp.pad(w_uq[0].reshape(MLA_Q_RANK, MLA_HEADS, qk), ((0, 0), (0, 0), (0, LANES - qk)))
    wuq = wuq.reshape(MLA_Q_RANK, MLA_HEADS * LANES).astype(BF16)
    wkv = w_ukv[0].reshape(MLA_KV_RANK, MLA_HEADS, MLA_NOPE_DIM + MLA_V_DIM)
    wk = jnp.pad(wkv[:, :, :MLA_NOPE_DIM], ((0, 0), (0, 0), (0, LANES - MLA_NOPE_DIM)))
    wukv = jnp.concatenate([wk.reshape(MLA_KV_RANK, MLA_HEADS * LANES),
                            wkv[:, :, MLA_NOPE_DIM:].reshape(MLA_KV_RANK, MLA_HEADS * MLA_V_DIM)], axis=1).astype(BF16)
    qg = (jnp.tile(q_norm[0], GQA_HEADS) * (HEAD_DIM ** -0.5 * LOG2E)).reshape(1, SZ_Q)
    kg = jnp.tile(k_norm[0], GQA_KV_HEADS).reshape(1, SZ_KV)
    cqg = cq_norm[0].reshape(1, MLA_Q_RANK)
    ckvg = ckv_norm[0].reshape(1, MLA_KV_RANK)
    cos64, sin64, cosm, sinm = _rope_tables(S)

    q_all, k_all, v_all = _inproj_call(x2d, win, wuq, wukv, _block_ones(SZ_Q, HEAD_DIM), _block_ones(SZ_KV, HEAD_DIM),
                                       qg, kg, cqg, ckvg, cos64, sin64, cosm, sinm, S)
    o_mix = _attn_call(q_all, k_all, v_all, B, S)

    wrh = w_router[0].astype(BF16)
    wrl = (w_router[0] - wrh.astype(F32)).astype(BF16)
    wr = jnp.concatenate([wrh, wrl, jnp.zeros((D, LANES - 2 * N_EXPERTS), BF16)], axis=1)
    x1, x1b, aff_t = _outproj_call(o_mix, x2d, w_out[0].astype(BF16), ln_attn_g[0].reshape(1, D),
                                   ln_attn_b[0].reshape(1, D), wr)

    pos, post, tab = _topk_call(aff_t, B, S, cap)
    tab = tab[:, :, :TAB_COLS].reshape(-1)
    xg, gs = _gather_call(tab, pos, aff_t, x1b, B, S, cap)
    yg = _ffn_call(xg, gs, w_gate[0], w_up[0], w_down[0])
    out = _final_call(tab, post, yg, x1, p[0].reshape(T, PLE_DIM), w_ple_proj[0].astype(BF16), w_ple_gate[0].astype(BF16),
                      ln_ffn_g[0].reshape(1, D), ln_ffn_b[0].reshape(1, D), ln_ple_g[0].reshape(1, D),
                      ln_ple_b[0].reshape(1, D), B, S, cap)
    return out.reshape(B, S, D)
```

```python
import functools

import numpy as np
import jax
import jax.numpy as jnp
from jax import lax
from jax.experimental import pallas as pl
from jax.experimental.pallas import tpu as pltpu

F32 = jnp.float32
BF16 = jnp.bfloat16

D_MODEL = 1024
HEAD_DIM = 64
GQA_HEADS = 8
GQA_KV_HEADS = 2
MLA_HEADS = 8
MLA_NOPE_DIM = 64
MLA_ROPE_DIM = 32
MLA_V_DIM = 64
MLA_Q_RANK = 256
MLA_KV_RANK = 256
N_EXPERTS = 16
CAPACITY_FACTOR = 2
EXPERT_FF = 2816
PLE_DIM = 256
GRID_W = 64
ROPE_BASE = 10000.0
LN_EPS = 1e-5
RMS_EPS = 1e-6
DEPTH = 1
DN_ALPHA = (2 * DEPTH) ** 0.25
LOG2E = 1.4426950408889634

LANES = 128
SZ_Q = GQA_HEADS * HEAD_DIM
SZ_KV = GQA_KV_HEADS * HEAD_DIM
IN_COLS = SZ_Q + 2 * SZ_KV + MLA_Q_RANK + MLA_KV_RANK + LANES
OFF_K = SZ_Q
OFF_V = OFF_K + SZ_KV
OFF_CQ = OFF_V + SZ_KV
OFF_CKV = OFF_CQ + MLA_Q_RANK
OFF_KR = OFF_CKV + MLA_KV_RANK
N_CHUNKS = 12
N_PAIRS = 8
QKV_COLS = N_CHUNKS * LANES
MIX_WIDTH = N_PAIRS * LANES

TM_IN = 1024
IN_SUB = 256
TQ = 256
PAIRS_PER_STEP = 2
ATTN_AHEAD = 2
TM_OUT = 1024
OUT_SUB = 1024
FF_TILE = 256
FFN_ROWS = 2048
FFN_RB = 1024
TS_FIN = 512
TOPK_SEQS = 8
TOK_BLK = 256
WIN = 64
WIN_ALIGN = 16
TAB_COLS = 16
TAB_FLAG = 8

VMEM_LIMIT = 56 * 1024 * 1024


def _cparams(n_axes, vmem=VMEM_LIMIT):
    return pltpu.CompilerParams(dimension_semantics=("arbitrary",) * n_axes, vmem_limit_bytes=vmem)


def _layer_norm(y, g, b):
    mu = jnp.mean(y, axis=-1, keepdims=True)
    yc = y - mu
    var = jnp.mean(yc * yc, axis=-1, keepdims=True)
    return yc * lax.rsqrt(var + LN_EPS) * g + b


def _split_bf16(v):
    hi = v.astype(BF16)
    lo = (v - hi.astype(F32)).astype(BF16)
    return hi, lo


def _rope(v, cos, sin, dist):
    lane = lax.broadcasted_iota(jnp.int32, v.shape, 1)
    first = (lane % (2 * dist)) < dist
    rot = jnp.where(first, pltpu.roll(v, LANES - dist, 1), pltpu.roll(v, dist, 1))
    return v * cos + rot * sin


def _head_sumsq(v, g_ref):
    hi, lo = _split_bf16(v * v)
    g = g_ref[...]
    return jnp.dot(hi, g, preferred_element_type=F32) + jnp.dot(lo, g, preferred_element_type=F32)


def _inproj_kernel(x_ref, win_ref, wuq_ref, wukv_ref, g512_ref, g128_ref, qg_ref, kg_ref, cqg_ref, ckvg_ref,
                   cos64_ref, sin64_ref, cosm_ref, sinm_ref, q_out, k_out, v_out):
    for r in range(x_ref.shape[0] // IN_SUB):
        _inproj_rows(slice(r * IN_SUB, (r + 1) * IN_SUB), x_ref, win_ref, wuq_ref, wukv_ref, g512_ref, g128_ref,
                     qg_ref, kg_ref, cqg_ref, ckvg_ref, cos64_ref, sin64_ref, cosm_ref, sinm_ref, q_out, k_out, v_out)


def _inproj_rows(rs, x_ref, win_ref, wuq_ref, wukv_ref, g512_ref, g128_ref, qg_ref, kg_ref, cqg_ref, ckvg_ref,
                 cos64_ref, sin64_ref, cosm_ref, sinm_ref, q_out, k_out, v_out):
    proj = jnp.dot(x_ref[rs, :].astype(BF16), win_ref[...], preferred_element_type=F32)
    cos64, sin64 = cos64_ref[rs, :], sin64_ref[rs, :]
    cosm, sinm = cosm_ref[rs, :], sinm_ref[rs, :]
    lane = lax.broadcasted_iota(jnp.int32, (proj.shape[0], LANES), 1)
    low = lane < HEAD_DIM

    qg = proj[:, :SZ_Q]
    qn = qg * lax.rsqrt(_head_sumsq(qg, g512_ref) * (1.0 / HEAD_DIM) + RMS_EPS) * qg_ref[...]
    for c in range(SZ_Q // LANES):
        sl = slice(c * LANES, (c + 1) * LANES)
        q_out[rs, sl] = _rope(qn[:, sl], cos64, sin64, HEAD_DIM // 4).astype(BF16)

    kg = proj[:, OFF_K:OFF_K + SZ_KV]
    kn = kg * lax.rsqrt(_head_sumsq(kg, g128_ref) * (1.0 / HEAD_DIM) + RMS_EPS) * kg_ref[...]
    kr = _rope(kn, cos64, sin64, HEAD_DIM // 4)
    vg = proj[:, OFF_V:OFF_V + SZ_KV]
    one_lo = jnp.where(lane == HEAD_DIM, 1.0, 0.0)
    one_hi = jnp.where(lane == 0, 1.0, 0.0)
    for src, dst, fill_lo, fill_hi in ((kr, k_out, 0.0, 0.0), (vg, v_out, one_lo, one_hi)):
        lo = jnp.where(low, src, fill_lo)
        hi = jnp.where(low, fill_hi, src)
        dst[rs, 0 * LANES:1 * LANES] = lo.astype(BF16)
        dst[rs, 1 * LANES:2 * LANES] = pltpu.roll(lo, HEAD_DIM, 1).astype(BF16)
        dst[rs, 2 * LANES:3 * LANES] = pltpu.roll(hi, HEAD_DIM, 1).astype(BF16)
        dst[rs, 3 * LANES:4 * LANES] = hi.astype(BF16)

    cq = proj[:, OFF_CQ:OFF_CQ + MLA_Q_RANK]
    cqn = cq * lax.rsqrt(jnp.sum(cq * cq, axis=-1, keepdims=True) * (1.0 / MLA_Q_RANK) + RMS_EPS) * cqg_ref[...]
    qm = jnp.dot(cqn.astype(BF16), wuq_ref[...], preferred_element_type=F32)
    mla_scale = (MLA_NOPE_DIM + MLA_ROPE_DIM) ** -0.5 * LOG2E
    for h in range(MLA_HEADS):
        sl = slice(h * LANES, (h + 1) * LANES)
        dsl = slice(SZ_Q + h * LANES, SZ_Q + (h + 1) * LANES)
        q_out[rs, dsl] = (_rope(qm[:, sl], cosm, sinm, MLA_ROPE_DIM // 4) * mla_scale).astype(BF16)

    ckv = proj[:, OFF_CKV:OFF_CKV + MLA_KV_RANK]
    ckvn = ckv * lax.rsqrt(jnp.sum(ckv * ckv, axis=-1, keepdims=True) * (1.0 / MLA_KV_RANK) + RMS_EPS) * ckvg_ref[...]
    kv = jnp.dot(ckvn.astype(BF16), wukv_ref[...], preferred_element_type=F32)
    kpe = _rope(proj[:, OFF_KR:OFF_KR + LANES], cosm, sinm, MLA_ROPE_DIM // 4)
    for h in range(MLA_HEADS):
        sl = slice(h * LANES, (h + 1) * LANES)
        dsl = slice(SZ_Q + h * LANES, SZ_Q + (h + 1) * LANES)
        k_out[rs, dsl] = (kv[:, sl] + kpe).astype(BF16)
    voff = MLA_HEADS * LANES
    for j in range(MLA_HEADS // 2):
        vp = kv[:, voff + j * LANES: voff + (j + 1) * LANES]
        base = SZ_Q + 2 * j * LANES
        v_out[rs, base:base + LANES] = jnp.where(low, vp, one_lo).astype(BF16)
        v_out[rs, base + LANES:base + 2 * LANES] = jnp.where(low, one_hi, vp).astype(BF16)


def _inproj_call(x2d, win, wuq, wukv, g512, g128, qg, kg, cqg, ckvg, cos64, sin64, cosm, sinm, seq):
    T = x2d.shape[0]
    tm = TM_IN
    tiles_per_seq = seq // tm
    full = lambda shape: pl.BlockSpec(shape, lambda i: (0,) * len(shape))
    tab = pl.BlockSpec((tm, LANES), lambda i: (i % tiles_per_seq, 0))
    out_spec = pl.BlockSpec((tm, QKV_COLS), lambda i: (i, 0))
    out_shape = jax.ShapeDtypeStruct((T, QKV_COLS), BF16)
    return pl.pallas_call(
        _inproj_kernel,
        grid=(T // tm,),
        in_specs=[pl.BlockSpec((tm, D_MODEL), lambda i: (i, 0)),
                  full(win.shape), full(wuq.shape), full(wukv.shape), full(g512.shape), full(g128.shape),
                  full(qg.shape), full(kg.shape), full(cqg.shape), full(ckvg.shape),
                  tab, tab, tab, tab],
        out_specs=[out_spec, out_spec, out_spec],
        out_shape=[out_shape, out_shape, out_shape],
        compiler_params=_cparams(1),
        name="inproj",
    )(x2d, win, wuq, wukv, g512, g128, qg, kg, cqg, ckvg, cos64, sin64, cosm, sinm)


def _attn_kernel(*refs):
    o_ref = refs[-1]
    seq = o_ref.shape[0]

    hk = seq // 2

    def probs(q, k_ref):
        nt = (((1,), (1,)), ((), ()))
        s = lax.dot_general(q, k_ref[...], nt, preferred_element_type=F32)
        return (jnp.exp2(s - jnp.max(s, axis=-1, keepdims=True)).astype(BF16),)

    def values(p, v_ref):
        return jnp.dot(p[0], v_ref[...], preferred_element_type=F32)

    low =lax.broadcasted_iota(jnp.int32, (TQ, LANES), 1) < HEAD_DIM
    even_out = {}

    def finish(p, v_ref, j, i, odd):
        o = values(p, v_ref)
        if not odd:
            even_out[(j, i)] = o[:, :LANES]
            return
        oe, oo = even_out.pop((j, i)), o[:, LANES:]
        out = jnp.where(low, oe / oe[:, HEAD_DIM:HEAD_DIM + 1], oo / oo[:, 0:1])
        o_ref[i * TQ:(i + 1) * TQ, j * LANES:(j + 1) * LANES] = out.astype(BF16)

    pending = []
    for j in range(PAIRS_PER_STEP):
        qe_ref, qo_ref, ke_ref, ko_ref, v_ref = refs[5 * j:5 * j + 5]
        for i in range(seq // TQ):
            for odd, q_ref, k_ref in ((0, qe_ref, ke_ref), (1, qo_ref, ko_ref)):
                p = probs(q_ref[i * TQ:(i + 1) * TQ, :], k_ref)
                if len(pending) == ATTN_AHEAD:
                    finish(*pending.pop(0))
                pending.append((p, v_ref, j, i, odd))
    for item in pending:
        finish(*item)


def _q_chunk(pr, odd):
    return jnp.where(pr < 4, pr, 4 + 2 * (pr - 4) + odd)


def _kv_chunk(pr, odd):
    return jnp.where(pr < 4, 2 * (pr // 2) + odd, 4 + 2 * (pr - 4) + odd)


def _attn_call(q_all, k_all, v_all, batch, seq):
    T = q_all.shape[0]
    blk = (seq, LANES)
    pps = PAIRS_PER_STEP
    in_specs, operands = [], []
    for j in range(pps):
        pair = lambda g, j=j: g * pps + j
        for odd in (0, 1):
            in_specs.append(pl.BlockSpec(blk, lambda b, g, odd=odd, pair=pair: (b, _q_chunk(pair(g), odd))))
            operands.append(q_all)
        for odd in (0, 1):
            in_specs.append(pl.BlockSpec(blk, lambda b, g, odd=odd, pair=pair: (b, _kv_chunk(pair(g), odd))))
            operands.append(k_all)
        in_specs.append(pl.BlockSpec((seq, 2 * LANES), lambda b, g, pair=pair: (b, _kv_chunk(pair(g), 0) // 2)))
        operands.append(v_all)
    return pl.pallas_call(
        _attn_kernel,
        grid=(batch, N_PAIRS // pps),
        in_specs=in_specs,
        out_specs=pl.BlockSpec((seq, pps * LANES), lambda b, g: (b, g)),
        out_shape=jax.ShapeDtypeStruct((T, MIX_WIDTH), BF16),
        compiler_params=_cparams(2),
        name="attention",
    )(*operands)


def _outproj_kernel(o_ref, x_ref, wout_ref, g_ref, b_ref, wr_ref, x1_ref, x1b_ref, aff_ref):
    for r in range(x_ref.shape[0] // OUT_SUB):
        rs = slice(r * OUT_SUB, (r + 1) * OUT_SUB)
        mix = jnp.dot(o_ref[rs, :], wout_ref[...], preferred_element_type=F32)
        x1 = _layer_norm(DN_ALPHA * x_ref[rs, :] + mix, g_ref[...], b_ref[...])
        x1_ref[rs, :] = x1
        hi, lo = _split_bf16(x1)
        x1b_ref[rs, :] = hi
        wr = wr_ref[...]
        la = jnp.dot(hi, wr, preferred_element_type=F32)
        lb = jnp.dot(lo, wr, preferred_element_type=F32)
        logits = (la + pltpu.roll(la, LANES - N_EXPERTS, 1) + lb).T[:N_EXPERTS, :]
        e = jnp.exp(logits - jnp.max(logits, axis=0, keepdims=True))
        aff_ref[:, rs] = e / jnp.sum(e, axis=0, keepdims=True)


def _outproj_call(o_mix, x2d, wout, g, b, wr):
    T = x2d.shape[0]
    tm = TM_OUT
    full = lambda shape: pl.BlockSpec(shape, lambda i: (0,) * len(shape))
    row = lambda w: pl.BlockSpec((tm, w), lambda i: (i, 0))
    return pl.pallas_call(
        _outproj_kernel,
        grid=(T // tm,),
        in_specs=[row(MIX_WIDTH), row(D_MODEL), full(wout.shape), full(g.shape), full(b.shape), full(wr.shape)],
        out_specs=[row(D_MODEL), row(D_MODEL), pl.BlockSpec((N_EXPERTS, tm), lambda i: (0, i))],
        out_shape=[jax.ShapeDtypeStruct((T, D_MODEL), F32), jax.ShapeDtypeStruct((T, D_MODEL), BF16),
                   jax.ShapeDtypeStruct((N_EXPERTS, T), F32)],
        compiler_params=_cparams(1),
        name="outproj_router",
    )(o_mix, x2d, wout, g, b, wr)


def _excl_cumsum(mask, upper):
    n = mask.shape[1]
    ones = jnp.where(mask, 1.0, 0.0)
    run = jnp.zeros((mask.shape[0], 1), F32)
    outs = []
    for j in range(n // LANES):
        blk = ones[:, j * LANES:(j + 1) * LANES]
        outs.append(jnp.dot(blk.astype(BF16), upper, preferred_element_type=F32) + run)
        run = run + jnp.sum(blk, axis=1, keepdims=True)
    return jnp.concatenate(outs, axis=1)


def _topk_kernel(cap, seq, aff_ref, pos_ref, post_ref, tab_ref):
    nb, n_e = pos_ref.shape[0], pos_ref.shape[1]
    aff_of = lambda j: aff_ref[:, j * seq:(j + 1) * seq]
    count = lambda mask: jnp.sum(jnp.where(mask, 1.0, 0.0), axis=1, keepdims=True)

    def step(k, ts):
        bit = jnp.left_shift(jnp.int32(1), 30 - k)
        out = []
        for j in range(nb):
            cand = ts[j] | bit
            cnt = count(aff_of(j) >= lax.bitcast_convert_type(cand, F32))
            out.append(jnp.where(cnt >= cap, cand, ts[j]))
        return tuple(out)

    ts = lax.fori_loop(0, 31, step, tuple(jnp.zeros((n_e, 1), jnp.int32) for _ in range(nb)))
    ri = lax.broadcasted_iota(jnp.int32, (LANES, LANES), 0)
    ci = lax.broadcasted_iota(jnp.int32, (LANES, LANES), 1)
    upper = jnp.where(ri < ci, 1.0, 0.0).astype(BF16)
    lane = lax.broadcasted_iota(jnp.int32, (n_e, LANES), 1)
    for j in range(nb):
        a = aff_of(j)
        above = a >= lax.bitcast_convert_type(ts[j] + 1, F32)
        window = jnp.logical_and(a >= lax.bitcast_convert_type(ts[j], F32), jnp.logical_not(above))

        def more(st):
            return jnp.max(st[1]) > 0.0

        def take_round(st):
            sel_f, rem, win_f = st
            win = win_f > 0.5
            top = jnp.max(jnp.where(win, a, -1.0), axis=1, keepdims=True)
            eq = jnp.logical_and(win, a == top)
            take = jnp.minimum(count(eq), rem)
            pick = jnp.logical_and(eq, _excl_cumsum(eq, upper) < take)
            return (jnp.where(pick, 1.0, sel_f), rem - take, jnp.where(eq, 0.0, win_f))

        sel_f, _, _ = lax.while_loop(more, take_round, (jnp.where(above, 1.0, 0.0), cap - count(above),
                                                         jnp.where(window, 1.0, 0.0)))
        sel = sel_f > 0.5
        pos = jnp.where(sel, _excl_cumsum(sel, upper), -1.0)
        pos_ref[j] = pos
        padded = jnp.concatenate([pos, jnp.full((LANES - n_e, seq), -1.0, F32)], axis=0)
        post_ref[j] = padded.T

        ones = jnp.where(sel, 1.0, 0.0)
        lo = jnp.zeros((n_e, 1), F32)
        tab = jnp.zeros((n_e, LANES), F32)
        over = jnp.zeros((n_e, 1), F32)
        for kb in range(seq // TOK_BLK):
            hi = lo + jnp.sum(ones[:, kb * TOK_BLK:(kb + 1) * TOK_BLK], axis=1, keepdims=True)
            win = jnp.minimum(jnp.floor(lo * (1.0 / WIN_ALIGN)) * WIN_ALIGN, float(cap - WIN))
            over = jnp.maximum(over, jnp.where(hi - win > WIN, 1.0, 0.0))
            tab = jnp.where(lane == kb, win, tab)
            lo = hi
        tab = jnp.where(lane == TAB_FLAG, jnp.max(over, axis=0, keepdims=True), tab)
        tab_ref[j] = tab.astype(jnp.int32)


def _topk_call(aff_t, batch, seq, cap):
    assert seq // TOK_BLK <= TAB_FLAG < TAB_COLS and cap % WIN == 0 and WIN % WIN_ALIGN == 0
    nb = TOPK_SEQS
    return pl.pallas_call(
        functools.partial(_topk_kernel, cap, seq),
        grid=(batch // nb,),
        in_specs=[pl.BlockSpec((N_EXPERTS, nb * seq), lambda b: (0, b))],
        out_specs=[pl.BlockSpec((nb, N_EXPERTS, seq), lambda b: (b, 0, 0)),
                   pl.BlockSpec((nb, seq, LANES), lambda b: (b, 0, 0)),
                   pl.BlockSpec((nb, N_EXPERTS, LANES), lambda b: (b, 0, 0))],
        out_shape=[jax.ShapeDtypeStruct((batch, N_EXPERTS, seq), F32),
                   jax.ShapeDtypeStruct((batch, seq, LANES), F32),
                   jax.ShapeDtypeStruct((batch, N_EXPERTS, LANES), jnp.int32)],
        compiler_params=_cparams(1),
        name="expert_choice",
    )(aff_t)


def _tab_at(tab_ref, b, e, col):
    return tab_ref[(b * N_EXPERTS + e) * TAB_COLS + col]


def _gather_kernel(cap, tab_ref, pos_ref, aff_ref, x_ref, xg_ref, gs_ref):
    b = pl.program_id(0)
    seq = x_ref.shape[0]
    fits = _tab_at(tab_ref, b, 0, TAB_FLAG) == 0

    @pl.when(fits)
    def _():
        xg_ref[...] = jnp.zeros_like(xg_ref)
        gs_ref[...] = jnp.zeros_like(gs_ref)
        wslot = lax.broadcasted_iota(jnp.int32, (WIN, 1), 0)
        for kb in range(seq // TOK_BLK):
            ts = slice(kb * TOK_BLK, (kb + 1) * TOK_BLK)
            wins, hots = [], []
            for e in range(N_EXPERTS):
                win = pl.multiple_of(_tab_at(tab_ref, b, e, kb), WIN_ALIGN)
                hit = pos_ref[0, e:e + 1, ts] == (wslot + win).astype(F32)
                g = jnp.sum(jnp.where(hit, aff_ref[e:e + 1, ts], 0.0), axis=1, keepdims=True)
                gs_ref[e, pl.ds(win, WIN), :] += jnp.broadcast_to(g, (WIN, LANES))
                wins.append(win)
                hots.append(jnp.where(hit, 1.0, 0.0).astype(BF16))
            rows = jnp.dot(jnp.concatenate(hots, axis=0), x_ref[ts, :], preferred_element_type=F32)
            for e in range(N_EXPERTS):
                xg_ref[e, pl.ds(wins[e], WIN), :] += rows[e * WIN:(e + 1) * WIN].astype(BF16)

    @pl.when(jnp.logical_not(fits))
    def _():
        slot = lax.broadcasted_iota(jnp.int32, (cap, seq), 0).astype(F32)

        def body(e, carry):
            onehot = pos_ref[0, pl.ds(e, 1), :] == slot
            xg = jnp.dot(jnp.where(onehot, 1.0, 0.0).astype(BF16), x_ref[...], preferred_element_type=F32)
            xg_ref[e] = xg.astype(BF16)
            g = jnp.sum(jnp.where(onehot, aff_ref[pl.ds(e, 1), :], 0.0), axis=1, keepdims=True)
            gs_ref[e] = jnp.broadcast_to(g, (cap, LANES))
            return carry

        lax.fori_loop(0, N_EXPERTS, body, 0)


def _gather_call(tab, pos, aff_t, x1b, batch, seq, cap):
    grid_spec = pltpu.PrefetchScalarGridSpec(
        num_scalar_prefetch=1,
        grid=(batch,),
        in_specs=[pl.BlockSpec((1, N_EXPERTS, seq), lambda b, tab: (b, 0, 0)),
                  pl.BlockSpec((N_EXPERTS, seq), lambda b, tab: (0, b)),
                  pl.BlockSpec((seq, D_MODEL), lambda b, tab: (b, 0))],
        out_specs=[pl.BlockSpec((N_EXPERTS, cap, D_MODEL), lambda b, tab: (0, b, 0)),
                   pl.BlockSpec((N_EXPERTS, cap, LANES), lambda b, tab: (0, b, 0))],
    )
    return pl.pallas_call(
        functools.partial(_gather_kernel, cap),
        grid_spec=grid_spec,
        out_shape=[jax.ShapeDtypeStruct((N_EXPERTS, batch * cap, D_MODEL), BF16),
                   jax.ShapeDtypeStruct((N_EXPERTS, batch * cap, LANES), F32)],
        compiler_params=_cparams(1),
        name="moe_gather",
    )(tab, pos, aff_t, x1b)


def _ffn_kernel(xg_ref, gs_ref, wg_ref, wu_ref, wd_ref, y_ref, acc_ref, wgb_ref, wub_ref, wdb_ref):
    f = pl.program_id(2)
    wgb_ref[...] = wg_ref[0].astype(BF16)
    wub_ref[...] = wu_ref[0].astype(BF16)
    wdb_ref[...] = wd_ref[0].astype(BF16)

    last = pl.num_programs(2) - 1

    def run(where):
        def body(i, carry):
            rows = pl.ds(pl.multiple_of(i * FFN_RB, FFN_RB), FFN_RB)
            xb = xg_ref[0, rows, :]
            g = jnp.dot(xb, wgb_ref[...], preferred_element_type=F32)
            u = jnp.dot(xb, wub_ref[...], preferred_element_type=F32)
            h = (g * jax.nn.sigmoid(g) * u).astype(BF16)
            y = jnp.dot(h, wdb_ref[...], preferred_element_type=F32)
            if where == "first":
                acc_ref[rows, :] = y
            elif where == "middle":
                acc_ref[rows, :] += y
            else:
                y_ref[0, rows, :] = ((acc_ref[rows, :] + y) * gs_ref[0, rows, :1]).astype(BF16)
            return carry

        lax.fori_loop(0, FFN_ROWS // FFN_RB, body, 0, unroll=True)

    pl.when(f == 0)(functools.partial(run, "first"))
    pl.when(jnp.logical_and(f != 0, f != last))(functools.partial(run, "middle"))
    pl.when(f == last)(functools.partial(run, "last"))


def _ffn_call(xg, gs, w_gate, w_up, w_down):
    n_e, rows, _ = xg.shape
    return pl.pallas_call(
        _ffn_kernel,
        grid=(n_e, rows // FFN_ROWS, EXPERT_FF // FF_TILE),
        in_specs=[pl.BlockSpec((1, FFN_ROWS, D_MODEL), lambda e, m, f: (e, m, 0)),
                  pl.BlockSpec((1, FFN_ROWS, LANES), lambda e, m, f: (e, m, 0)),
                  pl.BlockSpec((1, D_MODEL, FF_TILE), lambda e, m, f: (e, 0, f)),
                  pl.BlockSpec((1, D_MODEL, FF_TILE), lambda e, m, f: (e, 0, f)),
                  pl.BlockSpec((1, FF_TILE, D_MODEL), lambda e, m, f: (e, f, 0))],
        out_specs=pl.BlockSpec((1, FFN_ROWS, D_MODEL), lambda e, m, f: (e, m, 0)),
        out_shape=jax.ShapeDtypeStruct((n_e, rows, D_MODEL), BF16),
        scratch_shapes=[pltpu.VMEM((FFN_ROWS, D_MODEL), F32),
                        pltpu.VMEM((D_MODEL, FF_TILE), BF16),
                        pltpu.VMEM((D_MODEL, FF_TILE), BF16),
                        pltpu.VMEM((FF_TILE, D_MODEL), BF16)],
        compiler_params=_cparams(3),
        name="moe_ffn",
    )(xg, gs, w_gate, w_up, w_down)


def _final_kernel(cap, tab_ref, post_ref, yg_ref, x1_ref, p_ref, wpp_ref, wpg_ref, g2_ref, b2_ref, g3_ref, b3_ref,
                  out_ref, moe_ref):
    b = pl.program_id(0)
    ts = moe_ref.shape[0]
    fits = _tab_at(tab_ref, b, 0, TAB_FLAG) == 0
    per_tile = TOK_BLK // WIN

    @pl.when(fits)
    def _():
        lane = lax.broadcasted_iota(jnp.int32, (1, TOK_BLK), 1)
        grp = lane // WIN
        for j in range(ts // TOK_BLK):
            kb = pl.program_id(1) * (ts // TOK_BLK) + j
            post = post_ref[0, j * TOK_BLK:(j + 1) * TOK_BLK, :]
            acc = jnp.zeros((TOK_BLK, D_MODEL), F32)
            for e0 in range(0, N_EXPERTS, per_tile):
                wins = [pl.multiple_of(_tab_at(tab_ref, b, e0 + r, kb), WIN_ALIGN) for r in range(per_tile)]
                rhs = jnp.concatenate([yg_ref[e0 + r, pl.ds(wins[r], WIN), :] for r in range(per_tile)], axis=0)
                tgt = (lane - grp * WIN).astype(F32)
                col = jnp.zeros((TOK_BLK, TOK_BLK), F32)
                for r in range(per_tile):
                    tgt = tgt + jnp.where(grp == r, wins[r].astype(F32), 0.0)
                    col = jnp.where(grp == r, post[:, e0 + r:e0 + r + 1], col)
                onehot = jnp.where(col == tgt, 1.0, 0.0).astype(BF16)
                acc = acc + jnp.dot(onehot, rhs, preferred_element_type=F32)
            moe_ref[j * TOK_BLK:(j + 1) * TOK_BLK, :] = acc

    @pl.when(jnp.logical_not(fits))
    def _():
        slot = lax.broadcasted_iota(jnp.int32, (ts, cap), 1).astype(F32)

        def body(e, moe):
            lane_e = lax.broadcasted_iota(jnp.int32, (1, LANES), 1) == e
            col = jnp.sum(jnp.where(lane_e, post_ref[0], 0.0), axis=1, keepdims=True)
            onehot = jnp.where(col == slot, 1.0, 0.0).astype(BF16)
            return moe + jnp.dot(onehot, yg_ref[e], preferred_element_type=F32)

        moe_ref[...] = lax.fori_loop(0, N_EXPERTS, body, jnp.zeros((ts, D_MODEL), F32))

    x2 = _layer_norm(DN_ALPHA * x1_ref[...] + moe_ref[...], g2_ref[...], b2_ref[...])
    emb = jnp.dot(p_ref[...].astype(BF16), wpp_ref[...], preferred_element_type=F32)
    gate = jax.nn.sigmoid(jnp.dot(x2.astype(BF16), wpg_ref[...], preferred_element_type=F32))
    out_ref[...] = _layer_norm(DN_ALPHA * x2 + gate * emb, g3_ref[...], b3_ref[...])


def _final_call(tab, post, yg, x1, p2d, wpp, wpg, g2, b2, g3, b3, batch, seq, cap):
    T = x1.shape[0]
    ts = TS_FIN
    nt = seq // ts
    full = lambda shape: pl.BlockSpec(shape, lambda b, i, tab: (0,) * len(shape))
    row = lambda w: pl.BlockSpec((ts, w), lambda b, i, tab: (b * nt + i, 0))
    grid_spec = pltpu.PrefetchScalarGridSpec(
        num_scalar_prefetch=1,
        grid=(batch, nt),
        in_specs=[pl.BlockSpec((1, ts, LANES), lambda b, i, tab: (b, i, 0)),
                  pl.BlockSpec((N_EXPERTS, cap, D_MODEL), lambda b, i, tab: (0, b, 0)),
                  row(D_MODEL), row(PLE_DIM), full(wpp.shape), full(wpg.shape),
                  full(g2.shape), full(b2.shape), full(g3.shape), full(b3.shape)],
        out_specs=row(D_MODEL),
        scratch_shapes=[pltpu.VMEM((ts, D_MODEL), F32)],
    )
    return pl.pallas_call(
        functools.partial(_final_kernel, cap),
        grid_spec=grid_spec,
        out_shape=jax.ShapeDtypeStruct((T, D_MODEL), F32),
        compiler_params=_cparams(2),
        name="combine_ple",
    )(tab, post, yg, x1, p2d, wpp, wpg, g2, b2, g3, b3)


def _rope_tables(seq):
    f32 = np.float32
    rows = seq // GRID_W
    row = np.repeat(np.arange(rows, dtype=f32), GRID_W)
    col = np.tile(np.arange(GRID_W, dtype=f32), rows)

    def table(head_dim, lane_of):
        sub = head_dim // 2
        half = sub // 2
        inv_freq = f32(ROPE_BASE) ** (-np.arange(half, dtype=f32) * f32(2.0) / f32(sub))
        ang_row = row[:, None] * inv_freq[None, :]
        ang_col = col[:, None] * inv_freq[None, :]
        ang = np.concatenate([ang_row, ang_col], axis=1)
        idx = np.where(lane_of < 0, 0, (lane_of // sub) * half + (lane_of % sub) % half)
        first = (lane_of % sub) < half
        active = lane_of >= 0
        a = ang[:, idx]
        cos = np.where(active[None, :], np.cos(a), 1.0).astype(f32)
        sin = np.where(active[None, :], np.where(first[None, :], -np.sin(a), np.sin(a)), 0.0).astype(f32)
        return jnp.asarray(cos), jnp.asarray(sin)

    lanes = np.arange(LANES)
    cos64, sin64 = table(HEAD_DIM, lanes % HEAD_DIM)
    mla_lane = np.where((lanes >= MLA_NOPE_DIM) & (lanes < MLA_NOPE_DIM + MLA_ROPE_DIM), lanes - MLA_NOPE_DIM, -1)
    cosm, sinm = table(MLA_ROPE_DIM, mla_lane)
    return cos64, sin64, cosm, sinm


def _block_ones(n, blk):
    i = np.arange(n) // blk
    return jnp.asarray((i[:, None] == i[None, :]).astype(np.float32), dtype=BF16)


def kernel(x, p, w_in, q_norm, k_norm, cq_norm, ckv_norm, w_uq, w_ukv, w_out, ln_attn_g, ln_attn_b, w_router,
           w_gate, w_up, w_down, ln_ffn_g, ln_ffn_b, w_ple_proj, w_ple_gate, ln_ple_g, ln_ple_b):
    assert w_in.shape[0] == DEPTH
    B, S, D = x.shape
    T = B * S
    cap = CAPACITY_FACTOR * S // N_EXPERTS
    x2d = x.reshape(T, D)

    wi = w_in[0]
    zc = lambda n: jnp.zeros((D, n), F32)
    win = jnp.concatenate([wi[:, :OFF_KR], zc(MLA_NOPE_DIM), wi[:, OFF_KR:], zc(LANES - MLA_NOPE_DIM - MLA_ROPE_DIM)],
                          axis=1).astype(BF16)
    qk = MLA_NOPE_DIM + MLA_ROPE_DIM
    wuq = jnp.pad(w_uq[0].reshape(MLA_Q_RANK, MLA_HEADS, qk), ((0, 0), (0, 0), (0, LANES - qk)))
    wuq = wuq.reshape(MLA_Q_RANK, MLA_HEADS * LANES).astype(BF16)
    wkv = w_ukv[0].reshape(MLA_KV_RANK, MLA_HEADS, MLA_NOPE_DIM + MLA_V_DIM)
    wk = jnp.pad(wkv[:, :, :MLA_NOPE_DIM], ((0, 0), (0, 0), (0, LANES - MLA_NOPE_DIM)))
    wukv = jnp.concatenate([wk.reshape(MLA_KV_RANK, MLA_HEADS * LANES),
                            wkv[:, :, MLA_NOPE_DIM:].reshape(MLA_KV_RANK, MLA_HEADS * MLA_V_DIM)], axis=1).astype(BF16)
    qg = (jnp.tile(q_norm[0], GQA_HEADS) * (HEAD_DIM ** -0.5 * LOG2E)).reshape(1, SZ_Q)
    kg = jnp.tile(k_norm[0], GQA_KV_HEADS).reshape(1, SZ_KV)
    cqg = cq_norm[0].reshape(1, MLA_Q_RANK)
    ckvg = ckv_norm[0].reshape(1, MLA_KV_RANK)
    cos64, sin64, cosm, sinm = _rope_tables(S)

    q_all, k_all, v_all = _inproj_call(x2d, win, wuq, wukv, _block_ones(SZ_Q, HEAD_DIM), _block_ones(SZ_KV, HEAD_DIM),
                                       qg, kg, cqg, ckvg, cos64, sin64, cosm, sinm, S)
    o_mix = _attn_call(q_all, k_all, v_all, B, S)

    wrh = w_router[0].astype(BF16)
    wrl = (w_router[0] - wrh.astype(F32)).astype(BF16)
    wr = jnp.concatenate([wrh, wrl, jnp.zeros((D, LANES - 2 * N_EXPERTS), BF16)], axis=1)
    x1, x1b, aff_t = _outproj_call(o_mix, x2d, w_out[0].astype(BF16), ln_attn_g[0].reshape(1, D),
                                   ln_attn_b[0].reshape(1, D), wr)

    pos, post, tab = _topk_call(aff_t, B, S, cap)
    tab = tab[:, :, :TAB_COLS].reshape(-1)
    xg, gs = _gather_call(tab, pos, aff_t, x1b, B, S, cap)
    yg = _ffn_call(xg, gs, w_gate[0], w_up[0], w_down[0])
    out = _final_call(tab, post, yg, x1, p[0].reshape(T, PLE_DIM), w_ple_proj[0].astype(BF16), w_ple_gate[0].astype(BF16),
                      ln_ffn_g[0].reshape(1, D), ln_ffn_b[0].reshape(1, D), ln_ple_g[0].reshape(1, D),
                      ln_ple_b[0].reshape(1, D), B, S, cap)
    return out.reshape(B, S, D)
```
